```python
import jax, jax.numpy as jnp
from jax import lax
import numpy as np

D_MODEL = 1024
BATCH = 32
SEQ = 2048
DEPTH = 2

CHUNK = 64
HEAD_DIM = 64
D_RWKV = D_MODEL // 2
N_HEADS = D_RWKV // HEAD_DIM
D_CONV = D_MODEL // 2
DECAY_LORA = 64
ICLR_LORA = 64
VRES_LORA = 32
GATE_LORA = 128
CONV_WIDTH = 3
D_FF = 2816
RMS_EPS = 1e-6
GN_EPS = HEAD_DIM * 1e-5

SHIFT_SIZES = (D_RWKV, D_RWKV, D_RWKV, DECAY_LORA, ICLR_LORA, GATE_LORA)
REST_SIZES = (D_CONV, D_CONV, D_CONV, D_MODEL, D_MODEL)
SHIFT_COLS = sum(SHIFT_SIZES)
C_IN = SHIFT_COLS + sum(REST_SIZES)

kernel_name = "hybrid_rwkv7_shortconv_convffn"


def _split(t, sizes):
    return jnp.split(t, np.cumsum(sizes)[:-1].tolist(), axis=-1)


def rms_norm(x, g):
    xf = x.astype(jnp.float32)
    y = xf * lax.rsqrt(jnp.mean(xf * xf, axis=-1, keepdims=True) + RMS_EPS)
    return (y * g.astype(jnp.float32)).astype(x.dtype)


def token_shift_lerp(p, mu):
    p_prev = jnp.pad(p, ((0, 0), (1, 0), (0, 0)))[:, :-1]
    return p + (p_prev - p) * mu


def causal_dwconv(x, w, b=None):
    c = x.shape[-1]
    y = lax.conv_general_dilated(x, w[:, None, :].astype(x.dtype), window_strides=(1,),
                                 padding=[(CONV_WIDTH - 1, 0)],
                                 dimension_numbers=('NWC', 'WIO', 'NWC'),
                                 feature_group_count=c)
    return y if b is None else y + b


def heads(t):
    return t.reshape(t.shape[0], t.shape[1], N_HEADS, HEAD_DIM)


def wkv7_scan(r, decay, k, v, kk, a):
    bsz, _, h, n = r.shape
    def step(S, inp):
        r_t, w_t, k_t, v_t, kk_t, a_t = inp
        sa = jnp.einsum('bhvk,bhk->bhv', S, kk_t)
        S = (S * w_t[:, :, None, :]
             - sa[..., None] * (kk_t * a_t)[:, :, None, :]
             + v_t[..., None] * k_t[:, :, None, :])
        return S, jnp.einsum('bhvk,bhk->bhv', S, r_t)
    xs = tuple(jnp.moveaxis(t.astype(jnp.float32), 1, 0) for t in (r, decay, k, v, kk, a))
    _, o = lax.scan(step, jnp.zeros((bsz, h, n, n), jnp.float32), xs)
    return jnp.moveaxis(o, 0, 1)


def setup_inputs(seed: int = 0) -> dict:
    key = jax.random.key(seed)
    ks = iter(jax.random.split(key, 32))
    L, Lv, D, F = DEPTH, DEPTH - 1, D_MODEL, D_FF
    nrm = lambda shape, s: jax.random.normal(next(ks), shape, jnp.float32) * s
    uni = lambda shape, lo, hi: jax.random.uniform(next(ks), shape, jnp.float32, lo, hi)
    return {
        "x": nrm((BATCH, SEQ, D), 1.0),
        "norm_mix_g": 1.0 + nrm((L, D), 0.02),
        "w_in": nrm((L, D, C_IN), D ** -0.5),
        "mu_shift": uni((L, SHIFT_COLS), 0.0, 1.0),
        "w0": uni((L, D_RWKV), -4.0, 0.0),
        "decay_up": nrm((L, DECAY_LORA, D_RWKV), 0.1),
        "a0": nrm((L, D_RWKV), 0.1),
        "a_up": nrm((L, ICLR_LORA, D_RWKV), 0.1),
        "g_up": nrm((L, GATE_LORA, D_RWKV), GATE_LORA ** -0.5),
        "k_k": 0.85 + nrm((L, D_RWKV), 0.05),
        "k_a": 1.0 + nrm((L, D_RWKV), 0.05),
        "r_k": nrm((L, N_HEADS, HEAD_DIM), 0.1),
        "ln_x_w": 1.0 + nrm((L, D_RWKV), 0.02),
        "ln_x_b": nrm((L, D_RWKV), 0.02),
        "vres_down": nrm((Lv, D, VRES_LORA), D ** -0.5),
        "vres_up": nrm((Lv, VRES_LORA, D_RWKV), 0.1),
        "v0": nrm((Lv, D_RWKV), 0.1),
        "conv_w": nrm((L, CONV_WIDTH, D_CONV), 0.5),
        "proj_a": nrm((L, D_RWKV, D), D_RWKV ** -0.5),
        "proj_b": nrm((L, D_CONV, D), D_CONV ** -0.5),
        "w_out": nrm((L, D, D), D ** -0.5),
        "norm_ffn_g": 1.0 + nrm((L, D), 0.02),
        "w_up": nrm((L, D, 2 * F), D ** -0.5),
        "ffn_conv_w": nrm((L, CONV_WIDTH, 2 * F), 0.5),
        "ffn_conv_b": nrm((L, 2 * F), 0.02),
        "w_down": nrm((L, F, D), F ** -0.5),
        "norm_final_g": 1.0 + nrm((D,), 0.02),
    }


def reference(x, norm_mix_g, w_in, mu_shift, w0, decay_up, a0, a_up, g_up, k_k, k_a, r_k,
              ln_x_w, ln_x_b, vres_down, vres_up, v0, conv_w, proj_a, proj_b, w_out,
              norm_ffn_g, w_up, ffn_conv_w, ffn_conv_b, w_down, norm_final_g):
    bsz, seq, _ = x.shape
    v_first = None
    for l in range(DEPTH):
        h = rms_norm(x, norm_mix_g[l])
        p = h @ w_in[l]
        shifted = token_shift_lerp(p[..., :SHIFT_COLS], mu_shift[l])
        r, k, v, wd, ad, gd = _split(shifted, SHIFT_SIZES)
        u, gate_b, gate_c, merge_a, merge_b = _split(p[..., SHIFT_COLS:], REST_SIZES)

        w_log = -jax.nn.softplus(-(w0[l] + jnp.tanh(wd) @ decay_up[l])) - 0.5
        decay = jnp.exp(-jnp.exp(w_log.astype(jnp.float32)))
        a = jax.nn.sigmoid(a0[l] + ad @ a_up[l])
        g = jax.nn.sigmoid(gd) @ g_up[l]
        if l == 0:
            v_first = v
        else:
            v = v + (v_first - v) * jax.nn.sigmoid(v0[l - 1] + (h @ vres_down[l - 1]) @ vres_up[l - 1])
        kk = heads(k * k_k[l]).astype(jnp.float32)
        kk = kk / jnp.maximum(jnp.linalg.norm(kk, axis=-1, keepdims=True), 1e-12)
        k = k * (1.0 + (a - 1.0) * k_a[l])
        rh, kh, vh = heads(r), heads(k), heads(v)
        o = wkv7_scan(rh, heads(decay), kh, vh, kk, heads(a))
        mu = jnp.mean(o, axis=-1, keepdims=True)
        var = jnp.mean(jnp.square(o - mu), axis=-1, keepdims=True)
        o = ((o - mu) * lax.rsqrt(var + GN_EPS)).reshape(bsz, seq, D_RWKV)
        o = (o * ln_x_w[l].astype(jnp.float32) + ln_x_b[l].astype(jnp.float32)).astype(x.dtype)
        bonus = jnp.sum(rh * kh * r_k[l], axis=-1, keepdims=True) * vh
        y_a = (o + bonus.reshape(bsz, seq, D_RWKV)) * g

        y_b = gate_b * causal_dwconv(gate_c * u, conv_w[l])

        merged = (jax.nn.sigmoid(merge_a) * (y_a @ proj_a[l])
                  + jax.nn.sigmoid(merge_b) * (y_b @ proj_b[l]))
        x = x + merged @ w_out[l]

        h = rms_norm(x, norm_ffn_g[l])
        z = causal_dwconv(h @ w_up[l], ffn_conv_w[l], ffn_conv_b[l])
        z_gate, z_up = jnp.split(z, 2, axis=-1)
        x = x + (jax.nn.silu(z_gate) * z_up) @ w_down[l]
    return rms_norm(x, norm_final_g)
```

```python
import functools

import jax
import jax.numpy as jnp
from jax import lax
from jax.experimental import pallas as pl
from jax.experimental.pallas import tpu as pltpu

F32 = jnp.float32
BF16 = jnp.bfloat16

HEAD_DIM = 64
CHUNK = 64
PAIR = 2 * HEAD_DIM
DECAY_LORA = 64
ICLR_LORA = 64
GATE_LORA = 128
VRES_LORA = 32
LANE = 128
CARRY = 8
RMS_EPS = 1e-6
GN_EPS = HEAD_DIM * 1e-5
VMEM_LIMIT = 56 * 1024 * 1024


def _dot(a, b, precision=None):
    return jnp.dot(a, b, preferred_element_type=F32, precision=precision)


def _dot_nt(a, b, precision=None):
    return lax.dot_general(a, b, (((1,), (1,)), ((), ())), preferred_element_type=F32, precision=precision)


def _dot_tn(a, b, precision=None):
    return lax.dot_general(a, b, (((0,), (0,)), ((), ())), preferred_element_type=F32, precision=precision)


def _split_dot(x, ones_bf):
    hi = x.astype(BF16)
    lo = (x - hi.astype(F32)).astype(BF16)
    return _dot(hi, ones_bf) + _dot(lo, ones_bf)


def _rms(x, g):
    return x * lax.rsqrt(jnp.mean(x * x, axis=-1, keepdims=True) + RMS_EPS) * g


def _sigmoid(x):
    return 1.0 / (1.0 + jnp.exp(-x))


def _shift_rows(buf, val, t, tm):
    @pl.when(t == 0)
    def _():
        buf[0:CARRY, :] = jnp.zeros((CARRY, buf.shape[1]), F32)

    @pl.when(t > 0)
    def _():
        buf[0:CARRY, :] = buf[tm:tm + CARRY, :]

    buf[CARRY:tm + CARRY, :] = val
    return buf[CARRY - 1:tm + CARRY - 1, :], buf[CARRY - 2:tm + CARRY - 2, :]


def _mix_in_kernel(has_vres, tm, d_rwkv, *refs):
    if has_vres:
        (x_ref, g_ref, win_s_ref, win_r_ref, mu_ref, w0_ref, du_ref, a0_ref, au_ref, gu_ref, kk_ref, ka_ref,
         cw_ref, ones_ref, vd_ref, vu_ref, v0_ref, vf_ref,
         r_out, lw_out, k_out, v_out, kkn_out, kb_out, g_out, yb_out, ma_out, mb_out, sbuf, cbuf) = refs
    else:
        (x_ref, g_ref, win_s_ref, win_r_ref, mu_ref, w0_ref, du_ref, a0_ref, au_ref, gu_ref, kk_ref, ka_ref,
         cw_ref, ones_ref,
         r_out, lw_out, k_out, v_out, kkn_out, kb_out, g_out, yb_out, ma_out, mb_out, sbuf, cbuf) = refs
    t = pl.program_id(1)
    dr = d_rwkv
    h = _rms(x_ref[0], g_ref[...])
    hb = h.astype(BF16)

    ps = _dot(hb, win_s_ref[...])
    prev, _ = _shift_rows(sbuf, ps, t, tm)
    sh = ps + (prev - ps) * mu_ref[...]
    r = sh[:, 0:dr]
    k = sh[:, dr:2 * dr]
    v = sh[:, 2 * dr:3 * dr]
    wa = sh[:, 3 * dr:3 * dr + LANE]
    gd = sh[:, 3 * dr + LANE:3 * dr + 2 * LANE]

    y = -(w0_ref[...] + _dot(jnp.tanh(wa).astype(BF16), du_ref[...]))
    softplus = jnp.maximum(y, 0.0) + jnp.log(1.0 + jnp.exp(-jnp.abs(y)))
    lw_out[0] = -jnp.exp(-softplus - 0.5)
    a = _sigmoid(a0_ref[...] + _dot(wa.astype(BF16), au_ref[...]))
    g_out[0] = _dot(_sigmoid(gd).astype(BF16), gu_ref[...])
    if has_vres:
        lo = _dot(hb, vd_ref[...])
        mix = _sigmoid(v0_ref[...] + _dot(lo.astype(BF16), vu_ref[...]))
        v = v + (vf_ref[0] - v) * mix
    kk = k * kk_ref[...]
    nrm = jnp.sqrt(_split_dot(kk * kk, ones_ref[...]))
    kk = kk / jnp.maximum(nrm, 1e-12)
    r_out[0] = r
    k_out[0] = k * (1.0 + (a - 1.0) * ka_ref[...])
    v_out[0] = v
    kkn_out[0] = kk
    kb_out[0] = kk * a

    pr = _dot(hb, win_r_ref[...])
    u = pr[:, 0:dr]
    gate_b = pr[:, dr:2 * dr]
    gate_c = pr[:, 2 * dr:3 * dr]
    cu = gate_c * u
    c1, c2 = _shift_rows(cbuf, cu, t, tm)
    cw = cw_ref[...]
    yb_out[0] = gate_b * (cw[0:1, :] * c2 + cw[1:2, :] * c1 + cw[2:3, :] * cu)
    d = ma_out.shape[-1]
    ma_out[0] = _sigmoid(pr[:, 3 * dr:3 * dr + d])
    mb_out[0] = _sigmoid(pr[:, 3 * dr + d:3 * dr + 2 * d])


def _const_spec(shape):
    nd = len(shape)
    return pl.BlockSpec(shape, lambda *_: (0,) * nd)


def _mix_in(x, norm_g, win_s, win_r, mu, w0, du, a0, au, gu, k_k, k_a, conv_w, ones_h, vres, tm):
    b, t, d = x.shape
    dr = w0.shape[-1]
    has_vres = vres is not None
    row = lambda w: pl.BlockSpec((1, tm, w), lambda i, j: (i, j, 0))
    consts = [norm_g, win_s, win_r, mu, w0, du, a0, au, gu, k_k, k_a, conv_w, ones_h]
    in_specs = [row(d)] + [_const_spec(c.shape) for c in consts]
    args = [x] + consts
    if has_vres:
        vd, vu, v0, v_first = vres
        in_specs += [_const_spec(vd.shape), _const_spec(vu.shape), _const_spec(v0.shape), row(dr)]
        args += [vd, vu, v0, v_first]
    out_shape = [jax.ShapeDtypeStruct((b, t, dr), F32)] * 8 + [jax.ShapeDtypeStruct((b, t, d), F32)] * 2
    out_specs = [row(dr)] * 8 + [row(d)] * 2
    return pl.pallas_call(
        functools.partial(_mix_in_kernel, has_vres, tm, dr),
        grid=(b, t // tm),
        in_specs=in_specs,
        out_specs=out_specs,
        out_shape=out_shape,
        scratch_shapes=[pltpu.VMEM((tm + CARRY, win_s.shape[1]), F32), pltpu.VMEM((tm + CARRY, dr), F32)],
        compiler_params=pltpu.CompilerParams(dimension_semantics=("arbitrary", "arbitrary"),
                                             vmem_limit_bytes=VMEM_LIMIT),
        name="mix_in_vres" if has_vres else "mix_in",
    )(*args)


def _wkv_kernel(nc, prec, r_ref, lw_ref, k_ref, v_ref, kk_ref, kb_ref, g_ref, lnw_ref, lnb_ref, rk_ref, o_ref,
                g_s, gam_s, cm_s, pm_s, q_s, s0_s):
    L = CHUNK
    row = lax.broadcasted_iota(jnp.int32, (L, PAIR), 0)
    col = lax.broadcasted_iota(jnp.int32, (L, PAIR), 1)
    pos = col & (HEAD_DIM - 1)
    strict = pos < row
    incl = pos <= row
    eye = (pos == row).astype(F32)
    head0 = col < HEAD_DIM
    r2 = lax.broadcasted_iota(jnp.int32, (PAIR, PAIR), 0)
    c2 = lax.broadcasted_iota(jnp.int32, (PAIR, PAIR), 1)
    bdmask = (r2 >= HEAD_DIM) == (c2 >= HEAD_DIM)
    ones_bd = bdmask.astype(BF16)
    tr = lax.broadcasted_iota(jnp.int32, (L, L), 0)
    tc = lax.broadcasted_iota(jnp.int32, (L, L), 1)
    tri = (tc <= tr).astype(BF16)

    dot = functools.partial(_dot, precision=prec)
    dot_nt = functools.partial(_dot_nt, precision=prec)
    dot_tn = functools.partial(_dot_tn, precision=prec)

    def bd(x):
        return jnp.where(bdmask, jnp.concatenate([x, x], axis=0), 0.0)

    def packdiag(full):
        return jnp.where(head0, full[0:L, :], full[L:2 * L, :])

    def prep(c, _):
        sl = pl.ds(pl.multiple_of(c * L, L), L)
        r = r_ref[0, sl, :]
        lw = lw_ref[0, sl, :]
        k = k_ref[0, sl, :]
        v = v_ref[0, sl, :]
        kk = kk_ref[0, sl, :]
        kb = kb_ref[0, sl, :]
        hi = lw.astype(BF16)
        rem = lw - hi.astype(F32)
        mid = rem.astype(BF16)
        low = (rem - mid.astype(F32)).astype(BF16)
        cs = _dot(tri, hi) + _dot(tri, mid) + _dot(tri, low)
        cs_end = cs[L - 1:L, :]
        p_in = jnp.exp(cs)
        p_ex = jnp.exp(cs - lw)
        p_inv = jnp.exp(-cs)
        p_end = jnp.exp(cs_end - cs)
        kkd = kk * p_ex
        rd = r * p_in
        kh = k * p_inv
        bh = kb * p_inv
        ke = k * p_end
        be = kb * p_end

        lhs = jnp.concatenate([kkd, rd], axis=0)
        rhs_t = jnp.concatenate([jnp.where(head0, kh, 0.0), jnp.where(head0, 0.0, kh),
                                 jnp.where(head0, bh, 0.0), jnp.where(head0, 0.0, bh)], axis=0)
        aa = dot_nt(lhs, rhs_t)
        a1 = jnp.where(strict, aa[0:L, 0:PAIR], 0.0)
        a2 = jnp.where(strict, aa[0:L, PAIR:2 * PAIR], 0.0)
        a3 = jnp.where(incl, aa[L:2 * L, 0:PAIR], 0.0)
        a4 = jnp.where(incl, aa[L:2 * L, PAIR:2 * PAIR], 0.0)

        npow = dot(a2, bd(a2))
        tm = eye - a2
        for _i in range(4):
            res = dot(jnp.concatenate([tm, npow], axis=0), bd(npow))
            tm = tm + res[0:L, :]
            npow = res[L:2 * L, :]
        tm = tm + dot(tm, bd(npow))

        av = dot(jnp.concatenate([a1, a3], axis=0), bd(v))
        a1v = av[0:L, :]
        a3v = av[L:2 * L, :]
        wu = dot(tm, jnp.concatenate([bd(kkd), bd(a1v)], axis=1))
        w = wu[:, 0:PAIR]
        ut = wu[:, PAIR:2 * PAIR]
        aw = dot(a4, jnp.concatenate([bd(w), bd(ut)], axis=1))
        pm_s[c] = rd - aw[:, 0:PAIR]
        q_s[c] = a3v - aw[:, PAIR:2 * PAIR]
        g_s[c] = jnp.where(bdmask, -dot_tn(w, be), 0.0)
        cm_s[c] = packdiag(dot_tn(v, ke) - dot_tn(ut, be))
        gam_s[c] = jnp.exp(cs_end)
        return 0

    lax.fori_loop(0, nc, prep, 0)

    def scan(c, s):
        s0_s[c] = s
        return s * gam_s[c] + dot(s, g_s[c]) + cm_s[c]

    lax.fori_loop(0, nc, scan, jnp.zeros((L, PAIR), F32))

    def emit(c, _):
        sl = pl.ds(pl.multiple_of(c * L, L), L)
        o = dot_nt(pm_s[c], bd(s0_s[c])) + q_s[c]
        mean = _split_dot(o, ones_bd) * (1.0 / HEAD_DIM)
        dlt = o - mean
        var = _split_dot(dlt * dlt, ones_bd) * (1.0 / HEAD_DIM)
        on = dlt * lax.rsqrt(var + GN_EPS) * lnw_ref[...] + lnb_ref[...]
        r = r_ref[0, sl, :]
        bonus = _split_dot(r * k_ref[0, sl, :] * rk_ref[...], ones_bd) * v_ref[0, sl, :]
        o_ref[0, sl, :] = (on + bonus) * g_ref[0, sl, :]
        return 0

    lax.fori_loop(0, nc, emit, 0)


def _wkv(r, lw, k, v, kk, kb, g, ln_w, ln_b, r_k, prec):
    b, t, dr = r.shape
    nc = t // CHUNK
    seq = pl.BlockSpec((1, t, PAIR), lambda i, j: (i, 0, j))
    par = pl.BlockSpec((1, PAIR), lambda i, j: (0, j))
    chunk_buf = pltpu.VMEM((nc, CHUNK, PAIR), F32)
    return pl.pallas_call(
        functools.partial(_wkv_kernel, nc, prec),
        grid=(b, dr // PAIR),
        in_specs=[seq] * 7 + [par] * 3,
        out_specs=seq,
        out_shape=jax.ShapeDtypeStruct((b, t, dr), F32),
        scratch_shapes=[pltpu.VMEM((nc, PAIR, PAIR), F32), pltpu.VMEM((nc, 1, PAIR), F32),
                        chunk_buf, chunk_buf, chunk_buf, chunk_buf],
        compiler_params=pltpu.CompilerParams(dimension_semantics=("arbitrary", "arbitrary"),
                                             vmem_limit_bytes=VMEM_LIMIT),
        name="wkv",
    )(r, lw, k, v, kk, kb, g, ln_w, ln_b, r_k)


def _mix_out_kernel(x_ref, ya_ref, yb_ref, ma_ref, mb_ref, pa_ref, pb_ref, wo_ref, o_ref):
    merged = (ma_ref[...] * _dot(ya_ref[...].astype(BF16), pa_ref[...])
              + mb_ref[...] * _dot(yb_ref[...].astype(BF16), pb_ref[...]))
    o_ref[...] = x_ref[...] + _dot(merged.astype(BF16), wo_ref[...])


def _mix_out(x, ya, yb, ma, mb, pa, pb, wo, tm):
    n, d = x.shape
    dr = ya.shape[-1]
    row = lambda w: pl.BlockSpec((tm, w), lambda i: (i, 0))
    return pl.pallas_call(
        _mix_out_kernel,
        grid=(n // tm,),
        in_specs=[row(d), row(dr), row(dr), row(d), row(d),
                  _const_spec(pa.shape), _const_spec(pb.shape), _const_spec(wo.shape)],
        out_specs=row(d),
        out_shape=jax.ShapeDtypeStruct((n, d), F32),
        compiler_params=pltpu.CompilerParams(dimension_semantics=("arbitrary",), vmem_limit_bytes=VMEM_LIMIT),
        name="mix_out",
    )(x, ya, yb, ma, mb, pa, pb, wo)


def _ffn_kernel(final_norm, tm, n_split, x_ref, g_ref, wu_ref, cw_ref, cb_ref, wd_ref, gf_ref, o_ref, zbuf):
    t = pl.program_id(1)
    x = x_ref[0]
    hb = _rms(x, g_ref[...]).astype(BF16)
    f = wd_ref.shape[0]
    fs = f // n_split
    acc = x
    for i in range(n_split):
        halves = []
        for base in (i * fs, f + i * fs):
            z = _dot(hb, wu_ref[:, base:base + fs])
            buf = zbuf.at[:, base:base + fs]
            z1, z2 = _shift_rows(buf, z, t, tm)
            cw = cw_ref[:, base:base + fs]
            halves.append(cw[0:1, :] * z2 + cw[1:2, :] * z1 + cw[2:3, :] * z + cb_ref[:, base:base + fs])
        zg, zu = halves
        act = zg * _sigmoid(zg) * zu
        acc = acc + _dot(act.astype(BF16), wd_ref[i * fs:(i + 1) * fs, :])
    if final_norm:
        acc = _rms(acc, gf_ref[...])
    o_ref[0] = acc


def _ffn(x, norm_g, w_up, conv_w, conv_b, w_down, final_g, final_norm, tm, n_split):
    b, t, d = x.shape
    row = pl.BlockSpec((1, tm, d), lambda i, j: (i, j, 0))
    consts = [norm_g, w_up, conv_w, conv_b, w_down, final_g]
    return pl.pallas_call(
        functools.partial(_ffn_kernel, final_norm, tm, n_split),
        grid=(b, t // tm),
        in_specs=[row] + [_const_spec(c.shape) for c in consts],
        out_specs=row,
        out_shape=jax.ShapeDtypeStruct((b, t, d), F32),
        scratch_shapes=[pltpu.VMEM((tm + CARRY, w_up.shape[1]), F32)],
        compiler_params=pltpu.CompilerParams(dimension_semantics=("arbitrary", "arbitrary"),
                                             vmem_limit_bytes=VMEM_LIMIT),
        name="ffn_final" if final_norm else "ffn",
    )(x, *consts)


def _pad_rows(w, rows, offset=0):
    out = jnp.zeros((rows, w.shape[1]), w.dtype)
    return out.at[offset:offset + w.shape[0]].set(w)


def _forward(x, norm_mix_g, w_in, mu_shift, w0, decay_up, a0, a_up, g_up, k_k, k_a, r_k, ln_x_w, ln_x_b,
             vres_down, vres_up, v0, conv_w, proj_a, proj_b, w_out, norm_ffn_g, w_up, ffn_conv_w, ffn_conv_b,
             w_down, norm_final_g, *, tm_mix, tm_out, tm_ffn, ffn_split, prec):
    b, t, d = x.shape
    depth = w_in.shape[0]
    dr = w0.shape[-1]
    shift_cols = mu_shift.shape[-1]
    head = jnp.arange(dr) // HEAD_DIM
    ones_h = (head[:, None] == head[None, :]).astype(BF16)
    row = lambda p: p.reshape(1, -1)
    v_first = None
    for l in range(depth):
        win = w_in[l].astype(BF16)
        du = _pad_rows(decay_up[l], LANE).astype(BF16)
        au = _pad_rows(a_up[l], LANE, DECAY_LORA).astype(BF16)
        vres = None
        if l > 0:
            vd = jnp.pad(vres_down[l - 1], ((0, 0), (0, LANE - VRES_LORA))).astype(BF16)
            vu = _pad_rows(vres_up[l - 1], LANE).astype(BF16)
            vres = (vd, vu, row(v0[l - 1]), v_first)
        r, lw, k, v, kk, kb, g, yb, ma, mb = _mix_in(
            x, row(norm_mix_g[l]), win[:, :shift_cols], win[:, shift_cols:], row(mu_shift[l]), row(w0[l]), du,
            row(a0[l]), au, g_up[l].astype(BF16), row(k_k[l]), row(k_a[l]), conv_w[l], ones_h, vres, tm_mix)
        if l == 0:
            v_first = v
        ya = _wkv(r, lw, k, v, kk, kb, g, row(ln_x_w[l]), row(ln_x_b[l]), row(r_k[l]), prec)
        x = _mix_out(x.reshape(b * t, d), ya.reshape(b * t, dr), yb.reshape(b * t, dr), ma.reshape(b * t, d),
                     mb.reshape(b * t, d), proj_a[l].astype(BF16), proj_b[l].astype(BF16),
                     w_out[l].astype(BF16), tm_out).reshape(b, t, d)
        x = _ffn(x, row(norm_ffn_g[l]), w_up[l].astype(BF16), ffn_conv_w[l], row(ffn_conv_b[l]),
                 w_down[l].astype(BF16), row(norm_final_g), l == depth - 1, tm_ffn, ffn_split)
    return x


def kernel(x, norm_mix_g, w_in, mu_shift, w0, decay_up, a0, a_up, g_up, k_k, k_a, r_k, ln_x_w, ln_x_b, vres_down, vres_up, v0, conv_w, proj_a, proj_b, w_out, norm_ffn_g, w_up, ffn_conv_w, ffn_conv_b, w_down, norm_final_g):
    return _forward(x, norm_mix_g, w_in, mu_shift, w0, decay_up, a0, a_up, g_up, k_k, k_a, r_k, ln_x_w, ln_x_b,
                    vres_down, vres_up, v0, conv_w, proj_a, proj_b, w_out, norm_ffn_g, w_up, ffn_conv_w,
                    ffn_conv_b, w_down, norm_final_g,
                    tm_mix=256, tm_out=512, tm_ffn=256, ffn_split=2, prec=lax.Precision.HIGHEST)
```

```python
import functools

import jax
import jax.numpy as jnp
from jax import lax
from jax.experimental import pallas as pl
from jax.experimental.pallas import tpu as pltpu

F32 = jnp.float32
BF16 = jnp.bfloat16

HEAD_DIM = 64
CHUNK = 64
PAIR = 2 * HEAD_DIM
DECAY_LORA = 64
ICLR_LORA = 64
GATE_LORA = 128
VRES_LORA = 32
LANE = 128
CARRY = 8
RMS_EPS = 1e-6
GN_EPS = HEAD_DIM * 1e-5
VMEM_LIMIT = 56 * 1024 * 1024


def _dot(a, b):
    return jnp.dot(a, b, preferred_element_type=F32)


def _dot_nt(a, b):
    return lax.dot_general(a, b, (((1,), (1,)), ((), ())), preferred_element_type=F32)


def _dot_tn(a, b):
    return lax.dot_general(a, b, (((0,), (0,)), ((), ())), preferred_element_type=F32)


def _split_dot(x, ones_bf):
    hi = x.astype(BF16)
    lo = (x - hi.astype(F32)).astype(BF16)
    return _dot(hi, ones_bf) + _dot(lo, ones_bf)


def _rms(x, g):
    return x * lax.rsqrt(jnp.mean(x * x, axis=-1, keepdims=True) + RMS_EPS) * g


def _sigmoid(x):
    return 1.0 / (1.0 + jnp.exp(-x))


def _shift_rows(buf, val, t, tm):
    @pl.when(t == 0)
    def _():
        buf[0:CARRY, :] = jnp.zeros((CARRY, buf.shape[1]), F32)

    @pl.when(t > 0)
    def _():
        buf[0:CARRY, :] = buf[tm:tm + CARRY, :]

    buf[CARRY:tm + CARRY, :] = val
    return buf[CARRY - 1:tm + CARRY - 1, :], buf[CARRY - 2:tm + CARRY - 2, :]


def _mix_in_kernel(has_vres, tm, d_rwkv, *refs):
    if has_vres:
        (x_ref, g_ref, win_s_ref, win_r_ref, mu_ref, w0_ref, du_ref, a0_ref, au_ref, gu_ref, kk_ref, ka_ref,
         cw_ref, ones_ref, vd_ref, vu_ref, v0_ref, vf_ref,
         r_out, lw_out, k_out, v_out, kkn_out, kb_out, g_out, yb_out, ma_out, mb_out, sbuf, cbuf) = refs
    else:
        (x_ref, g_ref, win_s_ref, win_r_ref, mu_ref, w0_ref, du_ref, a0_ref, au_ref, gu_ref, kk_ref, ka_ref,
         cw_ref, ones_ref,
         r_out, lw_out, k_out, v_out, kkn_out, kb_out, g_out, yb_out, ma_out, mb_out, sbuf, cbuf) = refs
    t = pl.program_id(1)
    dr = d_rwkv
    h = _rms(x_ref[0], g_ref[...])
    hb = h.astype(BF16)

    ps = _dot(hb, win_s_ref[...])
    prev, _ = _shift_rows(sbuf, ps, t, tm)
    sh = ps + (prev - ps) * mu_ref[...]
    r = sh[:, 0:dr]
    k = sh[:, dr:2 * dr]
    v = sh[:, 2 * dr:3 * dr]
    wa = sh[:, 3 * dr:3 * dr + LANE]
    gd = sh[:, 3 * dr + LANE:3 * dr + 2 * LANE]

    y = -(w0_ref[...] + _dot(jnp.tanh(wa).astype(BF16), du_ref[...]))
    softplus = jnp.maximum(y, 0.0) + jnp.log(1.0 + jnp.exp(-jnp.abs(y)))
    lw_out[0] = -jnp.exp(-softplus - 0.5)
    a = _sigmoid(a0_ref[...] + _dot(wa.astype(BF16), au_ref[...]))
    g_out[0] = _dot(_sigmoid(gd).astype(BF16), gu_ref[...])
    if has_vres:
        lo = _dot(hb, vd_ref[...])
        mix = _sigmoid(v0_ref[...] + _dot(lo.astype(BF16), vu_ref[...]))
        v = v + (vf_ref[0] - v) * mix
    kk = k * kk_ref[...]
    nrm = jnp.sqrt(_split_dot(kk * kk, ones_ref[...]))
    kk = kk / jnp.maximum(nrm, 1e-12)
    r_out[0] = r
    k_out[0] = k * (1.0 + (a - 1.0) * ka_ref[...])
    v_out[0] = v
    kkn_out[0] = kk
    kb_out[0] = kk * a

    pr = _dot(hb, win_r_ref[...])
    u = pr[:, 0:dr]
    gate_b = pr[:, dr:2 * dr]
    gate_c = pr[:, 2 * dr:3 * dr]
    cu = gate_c * u
    c1, c2 = _shift_rows(cbuf, cu, t, tm)
    cw = cw_ref[...]
    yb_out[0] = gate_b * (cw[0:1, :] * c2 + cw[1:2, :] * c1 + cw[2:3, :] * cu)
    d = ma_out.shape[-1]
    ma_out[0] = _sigmoid(pr[:, 3 * dr:3 * dr + d])
    mb_out[0] = _sigmoid(pr[:, 3 * dr + d:3 * dr + 2 * d])


def _const_spec(shape):
    nd = len(shape)
    return pl.BlockSpec(shape, lambda *_: (0,) * nd)


def _mix_in(x, norm_g, win_s, win_r, mu, w0, du, a0, au, gu, k_k, k_a, conv_w, ones_h, vres, tm):
    b, t, d = x.shape
    dr = w0.shape[-1]
    has_vres = vres is not None
    row = lambda w: pl.BlockSpec((1, tm, w), lambda i, j: (i, j, 0))
    consts = [norm_g, win_s, win_r, mu, w0, du, a0, au, gu, k_k, k_a, conv_w, ones_h]
    in_specs = [row(d)] + [_const_spec(c.shape) for c in consts]
    args = [x] + consts
    if has_vres:
        vd, vu, v0, v_first = vres
        in_specs += [_const_spec(vd.shape), _const_spec(vu.shape), _const_spec(v0.shape), row(dr)]
        args += [vd, vu, v0, v_first]
    out_shape = [jax.ShapeDtypeStruct((b, t, dr), F32)] * 8 + [jax.ShapeDtypeStruct((b, t, d), F32)] * 2
    out_specs = [row(dr)] * 8 + [row(d)] * 2
    return pl.pallas_call(
        functools.partial(_mix_in_kernel, has_vres, tm, dr),
        grid=(b, t // tm),
        in_specs=in_specs,
        out_specs=out_specs,
        out_shape=out_shape,
        scratch_shapes=[pltpu.VMEM((tm + CARRY, win_s.shape[1]), F32), pltpu.VMEM((tm + CARRY, dr), F32)],
        compiler_params=pltpu.CompilerParams(dimension_semantics=("arbitrary", "arbitrary"),
                                             vmem_limit_bytes=VMEM_LIMIT),
        name="mix_in_vres" if has_vres else "mix_in",
    )(*args)


def _wkv_kernel(nc, prep_group, emit_group, r_ref, lw_ref, k_ref, v_ref, kk_ref, kb_ref, g_ref, lnw_ref, lnb_ref,
                rk_ref, o_ref, g_s, gam_s, cm_s, pm_s, q_s):
    L = CHUNK
    row = lax.broadcasted_iota(jnp.int32, (L, PAIR), 0)
    col = lax.broadcasted_iota(jnp.int32, (L, PAIR), 1)
    pos = col & (HEAD_DIM - 1)
    strict = pos < row
    incl = pos <= row
    eye = (pos == row).astype(F32)
    head0 = col < HEAD_DIM
    r2 = lax.broadcasted_iota(jnp.int32, (PAIR, PAIR), 0)
    c2 = lax.broadcasted_iota(jnp.int32, (PAIR, PAIR), 1)
    bdmask = (r2 >= HEAD_DIM) == (c2 >= HEAD_DIM)
    ones_bd = bdmask.astype(BF16)
    tr = lax.broadcasted_iota(jnp.int32, (L, L), 0)
    tc = lax.broadcasted_iota(jnp.int32, (L, L), 1)
    tri = (tc <= tr).astype(BF16)

    dot = lambda a, b: _dot(a.astype(BF16), b.astype(BF16))
    dot_nt = lambda a, b: _dot_nt(a.astype(BF16), b.astype(BF16))
    dot_tn = lambda a, b: _dot_tn(a.astype(BF16), b.astype(BF16))

    def bd(x):
        return jnp.where(bdmask, jnp.concatenate([x, x], axis=0), 0.0)

    def packdiag(full):
        return jnp.where(head0, full[0:L, :], full[L:2 * L, :])

    def each(f, *lists):
        return [f(*xs) for xs in zip(*lists)]

    def prep(i, _):
        cs_idx = [i * prep_group + j for j in range(prep_group)]
        sls = [pl.ds(pl.multiple_of(c * L, L), L) for c in cs_idx]
        r = [r_ref[0, sl, :] for sl in sls]
        lw = [lw_ref[0, sl, :] for sl in sls]
        k = [k_ref[0, sl, :] for sl in sls]
        v = [v_ref[0, sl, :] for sl in sls]
        kk = [kk_ref[0, sl, :] for sl in sls]
        kb = [kb_ref[0, sl, :] for sl in sls]

        def cumsum(x):
            hi = x.astype(BF16)
            rem = x - hi.astype(F32)
            mid = rem.astype(BF16)
            low = (rem - mid.astype(F32)).astype(BF16)
            return _dot(tri, hi) + _dot(tri, mid) + _dot(tri, low)

        cs = each(cumsum, lw)
        cs_end = each(lambda x: x[L - 1:L, :], cs)
        kkd = each(lambda kk_, cs_, lw_: kk_ * jnp.exp(cs_ - lw_), kk, cs, lw)
        rd = each(lambda r_, cs_: r_ * jnp.exp(cs_), r, cs)
        p_inv = each(lambda cs_: jnp.exp(-cs_), cs)
        p_end = each(lambda ce, cs_: jnp.exp(ce - cs_), cs_end, cs)
        kh = each(lambda a, b: a * b, k, p_inv)
        bh = each(lambda a, b: a * b, kb, p_inv)
        ke = each(lambda a, b: a * b, k, p_end)
        be = each(lambda a, b: a * b, kb, p_end)

        def scores(kkd_, rd_, kh_, bh_):
            lhs = jnp.concatenate([kkd_, rd_], axis=0)
            rhs_t = jnp.concatenate([jnp.where(head0, kh_, 0.0), jnp.where(head0, 0.0, kh_),
                                     jnp.where(head0, bh_, 0.0), jnp.where(head0, 0.0, bh_)], axis=0)
            return dot_nt(lhs, rhs_t)

        aa = each(scores, kkd, rd, kh, bh)
        a1 = each(lambda x: jnp.where(strict, x[0:L, 0:PAIR], 0.0), aa)
        a2 = each(lambda x: jnp.where(strict, x[0:L, PAIR:2 * PAIR], 0.0), aa)
        a3 = each(lambda x: jnp.where(incl, x[L:2 * L, 0:PAIR], 0.0), aa)
        a4 = each(lambda x: jnp.where(incl, x[L:2 * L, PAIR:2 * PAIR], 0.0), aa)

        npow = each(lambda n: dot(n, bd(n)), a2)
        tm = each(lambda n: eye - n, a2)
        for _i in range(4):
            res = each(lambda t_, n: dot(jnp.concatenate([t_, n], axis=0), bd(n)), tm, npow)
            tm = each(lambda t_, x: t_ + x[0:L, :], tm, res)
            npow = each(lambda x: x[L:2 * L, :], res)
        tm = each(lambda t_, n: t_ + dot(t_, bd(n)), tm, npow)

        av = each(lambda x1, x3, v_: dot(jnp.concatenate([x1, x3], axis=0), bd(v_)), a1, a3, v)
        wu = each(lambda t_, kkd_, av_: dot(t_, jnp.concatenate([bd(kkd_), bd(av_[0:L, :])], axis=1)), tm, kkd, av)
        aw = each(lambda x4, wu_: dot(x4, jnp.concatenate([bd(wu_[:, 0:PAIR]), bd(wu_[:, PAIR:2 * PAIR])], axis=1)),
                  a4, wu)
        gm = each(lambda wu_, be_: dot_tn(wu_[:, 0:PAIR], be_), wu, be)
        cm = each(lambda v_, ke_, wu_, be_: dot_tn(v_, ke_) - dot_tn(wu_[:, PAIR:2 * PAIR], be_), v, ke, wu, be)
        for j, c in enumerate(cs_idx):
            pm_s[c] = (rd[j] - aw[j][:, 0:PAIR]).astype(BF16)
            q_s[c] = av[j][L:2 * L, :] - aw[j][:, PAIR:2 * PAIR]
            g_s[c] = jnp.where(bdmask, -gm[j], 0.0).astype(BF16)
            cm_s[c] = packdiag(cm[j])
            gam_s[c] = jnp.exp(cs_end[j])
        return 0

    lax.fori_loop(0, nc // prep_group, prep, 0)

    def emit(i, s):
        cs_idx = [i * emit_group + j for j in range(emit_group)]
        sls = [pl.ds(pl.multiple_of(c * L, L), L) for c in cs_idx]
        sb = []
        for c in cs_idx:
            sb.append(s.astype(BF16))
            s = s * gam_s[c] + _dot(sb[-1], g_s[c]) + cm_s[c]
        o = [_dot_nt(pm_s[c], bd(sb_)) + q_s[c] for c, sb_ in zip(cs_idx, sb)]
        mean = each(lambda o_: _split_dot(o_, ones_bd) * (1.0 / HEAD_DIM), o)
        dlt = each(lambda o_, m: o_ - m, o, mean)
        var = each(lambda d: _split_dot(d * d, ones_bd) * (1.0 / HEAD_DIM), dlt)
        bonus = [_split_dot(r_ref[0, sl, :] * k_ref[0, sl, :] * rk_ref[...], ones_bd) * v_ref[0, sl, :] for sl in sls]
        for sl, d, vr, bo in zip(sls, dlt, var, bonus):
            on = d * lax.rsqrt(vr + GN_EPS) * lnw_ref[...] + lnb_ref[...]
            o_ref[0, sl, :] = (on + bo) * g_ref[0, sl, :]
        return s

    lax.fori_loop(0, nc // emit_group, emit, jnp.zeros((L, PAIR), F32))


def _wkv(r, lw, k, v, kk, kb, g, ln_w, ln_b, r_k, prep_group, emit_group):
    b, t, dr = r.shape
    nc = t // CHUNK
    seq = pl.BlockSpec((1, t, PAIR), lambda i, j: (i, 0, j))
    par = pl.BlockSpec((1, PAIR), lambda i, j: (0, j))
    return pl.pallas_call(
        functools.partial(_wkv_kernel, nc, prep_group, emit_group),
        grid=(b, dr // PAIR),
        in_specs=[seq] * 7 + [par] * 3,
        out_specs=seq,
        out_shape=jax.ShapeDtypeStruct((b, t, dr), F32),
        scratch_shapes=[pltpu.VMEM((nc, PAIR, PAIR), BF16),
                        pltpu.VMEM((nc, 1, PAIR), F32),
                        pltpu.VMEM((nc, CHUNK, PAIR), F32),
                        pltpu.VMEM((nc, CHUNK, PAIR), BF16),
                        pltpu.VMEM((nc, CHUNK, PAIR), F32)],
        compiler_params=pltpu.CompilerParams(dimension_semantics=("arbitrary", "arbitrary"),
                                             vmem_limit_bytes=VMEM_LIMIT),
        name="wkv",
    )(r, lw, k, v, kk, kb, g, ln_w, ln_b, r_k)


def _mix_out_kernel(x_ref, ya_ref, yb_ref, ma_ref, mb_ref, pa_ref, pb_ref, wo_ref, o_ref):
    merged = (ma_ref[...] * _dot(ya_ref[...].astype(BF16), pa_ref[...])
              + mb_ref[...] * _dot(yb_ref[...].astype(BF16), pb_ref[...]))
    o_ref[...] = x_ref[...] + _dot(merged.astype(BF16), wo_ref[...])


def _mix_out(x, ya, yb, ma, mb, pa, pb, wo, tm):
    n, d = x.shape
    dr = ya.shape[-1]
    row = lambda w: pl.BlockSpec((tm, w), lambda i: (i, 0))
    return pl.pallas_call(
        _mix_out_kernel,
        grid=(n // tm,),
        in_specs=[row(d), row(dr), row(dr), row(d), row(d),
                  _const_spec(pa.shape), _const_spec(pb.shape), _const_spec(wo.shape)],
        out_specs=row(d),
        out_shape=jax.ShapeDtypeStruct((n, d), F32),
        compiler_params=pltpu.CompilerParams(dimension_semantics=("arbitrary",), vmem_limit_bytes=VMEM_LIMIT),
        name="mix_out",
    )(x, ya, yb, ma, mb, pa, pb, wo)


def _ffn_kernel(final_norm, tm, n_split, x_ref, g_ref, wu_ref, cw_ref, cb_ref, wd_ref, gf_ref, o_ref, zbuf):
    t = pl.program_id(1)
    x = x_ref[0]
    hb = _rms(x, g_ref[...]).astype(BF16)
    f = wd_ref.shape[0]
    fs = f // n_split
    acc = x
    for i in range(n_split):
        halves = []
        for base in (i * fs, f + i * fs):
            z = _dot(hb, wu_ref[:, base:base + fs])
            buf = zbuf.at[:, base:base + fs]
            z1, z2 = _shift_rows(buf, z, t, tm)
            cw = cw_ref[:, base:base + fs]
            halves.append(cw[0:1, :] * z2 + cw[1:2, :] * z1 + cw[2:3, :] * z + cb_ref[:, base:base + fs])
        zg, zu = halves
        act = zg * _sigmoid(zg) * zu
        acc = acc + _dot(act.astype(BF16), wd_ref[i * fs:(i + 1) * fs, :])
    if final_norm:
        acc = _rms(acc, gf_ref[...])
    o_ref[0] = acc


def _ffn(x, norm_g, w_up, conv_w, conv_b, w_down, final_g, final_norm, tm, n_split):
    b, t, d = x.shape
    row = pl.BlockSpec((1, tm, d), lambda i, j: (i, j, 0))
    consts = [norm_g, w_up, conv_w, conv_b, w_down, final_g]
    return pl.pallas_call(
        functools.partial(_ffn_kernel, final_norm, tm, n_split),
        grid=(b, t // tm),
        in_specs=[row] + [_const_spec(c.shape) for c in consts],
        out_specs=row,
        out_shape=jax.ShapeDtypeStruct((b, t, d), F32),
        scratch_shapes=[pltpu.VMEM((tm + CARRY, w_up.shape[1]), F32)],
        compiler_params=pltpu.CompilerParams(dimension_semantics=("arbitrary", "arbitrary"),
                                             vmem_limit_bytes=VMEM_LIMIT),
        name="ffn_final" if final_norm else "ffn",
    )(x, *consts)


def _pad_rows(w, rows, offset=0):
    out = jnp.zeros((rows, w.shape[1]), w.dtype)
    return out.at[offset:offset + w.shape[0]].set(w)


def _forward(x, norm_mix_g, w_in, mu_shift, w0, decay_up, a0, a_up, g_up, k_k, k_a, r_k, ln_x_w, ln_x_b,
             vres_down, vres_up, v0, conv_w, proj_a, proj_b, w_out, norm_ffn_g, w_up, ffn_conv_w, ffn_conv_b,
             w_down, norm_final_g, *, tm_mix, tm_out, tm_ffn, ffn_split, wkv_group):
    b, t, d = x.shape
    depth = w_in.shape[0]
    dr = w0.shape[-1]
    shift_cols = mu_shift.shape[-1]
    head = jnp.arange(dr) // HEAD_DIM
    ones_h = (head[:, None] == head[None, :]).astype(BF16)
    row = lambda p: p.reshape(1, -1)
    v_first = None
    for l in range(depth):
        win = w_in[l].astype(BF16)
        du = _pad_rows(decay_up[l], LANE).astype(BF16)
        au = _pad_rows(a_up[l], LANE, DECAY_LORA).astype(BF16)
        vres = None
        if l > 0:
            vd = jnp.pad(vres_down[l - 1], ((0, 0), (0, LANE - VRES_LORA))).astype(BF16)
            vu = _pad_rows(vres_up[l - 1], LANE).astype(BF16)
            vres = (vd, vu, row(v0[l - 1]), v_first)
        r, lw, k, v, kk, kb, g, yb, ma, mb = _mix_in(
            x, row(norm_mix_g[l]), win[:, :shift_cols], win[:, shift_cols:], row(mu_shift[l]), row(w0[l]), du,
            row(a0[l]), au, g_up[l].astype(BF16), row(k_k[l]), row(k_a[l]), conv_w[l], ones_h, vres, tm_mix)
        if l == 0:
            v_first = v
        ya = _wkv(r, lw, k, v, kk, kb, g, row(ln_x_w[l]), row(ln_x_b[l]), row(r_k[l]), *wkv_group)
        x = _mix_out(x.reshape(b * t, d), ya.reshape(b * t, dr), yb.reshape(b * t, dr), ma.reshape(b * t, d),
                     mb.reshape(b * t, d), proj_a[l].astype(BF16), proj_b[l].astype(BF16),
                     w_out[l].astype(BF16), tm_out).reshape(b, t, d)
        x = _ffn(x, row(norm_ffn_g[l]), w_up[l].astype(BF16), ffn_conv_w[l], row(ffn_conv_b[l]),
                 w_down[l].astype(BF16), row(norm_final_g), l == depth - 1, tm_ffn, ffn_split)
    return x


def kernel(x, norm_mix_g, w_in, mu_shift, w0, decay_up, a0, a_up, g_up, k_k, k_a, r_k, ln_x_w, ln_x_b, vres_down, vres_up, v0, conv_w, proj_a, proj_b, w_out, norm_ffn_g, w_up, ffn_conv_w, ffn_conv_b, w_down, norm_final_g):
    return _forward(x, norm_mix_g, w_in, mu_shift, w0, decay_up, a0, a_up, g_up, k_k, k_a, r_k, ln_x_w, ln_x_b,
                    vres_down, vres_up, v0, conv_w, proj_a, proj_b, w_out, norm_ffn_g, w_up, ffn_conv_w,
                    ffn_conv_b, w_down, norm_final_g,
                    tm_mix=256, tm_out=512, tm_ffn=256, ffn_split=2, wkv_group=(8, 4))
```

```python
import functools

import jax
import jax.numpy as jnp
from jax import lax
from jax.experimental import pallas as pl
from jax.experimental.pallas import tpu as pltpu

F32 = jnp.float32
BF16 = jnp.bfloat16

HEAD_DIM = 64
CHUNK = 64
PAIR = 2 * HEAD_DIM
DECAY_LORA = 64
ICLR_LORA = 64
GATE_LORA = 128
VRES_LORA = 32
LANE = 128
CARRY = 8
RMS_EPS = 1e-6
GN_EPS = HEAD_DIM * 1e-5
VMEM_LIMIT = 56 * 1024 * 1024


def _dot(a, b):
    return jnp.dot(a, b, preferred_element_type=F32)


def _dot_nt(a, b):
    return lax.dot_general(a, b, (((1,), (1,)), ((), ())), preferred_element_type=F32)


def _dot_tn(a, b):
    return lax.dot_general(a, b, (((0,), (0,)), ((), ())), preferred_element_type=F32)


def _hi_lo(x):
    hi = x.astype(BF16)
    return hi, (x - hi.astype(F32)).astype(BF16)


def _split_dot(x, ones_bf):
    hi, lo = _hi_lo(x)
    return _dot(hi, ones_bf) + _dot(lo, ones_bf)


def _rms(x, g):
    return x * lax.rsqrt(jnp.mean(x * x, axis=-1, keepdims=True) + RMS_EPS) * g


def _sigmoid(x):
    return 1.0 / (1.0 + jnp.exp(-x))


def _prev_rows(z, carry_ref, lo, hi):
    tm = z.shape[0]
    prev = carry_ref[:, lo:hi]
    carry_ref[:, lo:hi] = z[tm - CARRY:tm, :]
    row = lax.broadcasted_iota(jnp.int32, prev.shape, 0)
    out = []
    for shift in (1, 2):
        rolled = pltpu.roll(z, shift, 0)
        first = jnp.where(row < shift, pltpu.roll(prev, shift, 0), rolled[0:CARRY, :])
        out.append(jnp.concatenate([first, rolled[CARRY:, :]], axis=0))
    return out


def _mix_in_kernel(has_vres, d_rwkv, *refs):
    if has_vres:
        (x_ref, g_ref, win_s_ref, win_r_ref, mu_ref, w0_ref, du_ref, a0_ref, au_ref, gu_ref, kk_ref, ka_ref,
         cw_ref, ones_ref, vd_ref, vu_ref, v0_ref, vf_ref,
         r_out, lw_out, k_out, v_out, kkn_out, kb_out, g_out, yb_out, ma_out, mb_out, scar, ccar) = refs
    else:
        (x_ref, g_ref, win_s_ref, win_r_ref, mu_ref, w0_ref, du_ref, a0_ref, au_ref, gu_ref, kk_ref, ka_ref,
         cw_ref, ones_ref,
         r_out, lw_out, k_out, v_out, kkn_out, kb_out, g_out, yb_out, ma_out, mb_out, scar, ccar) = refs

    @pl.when(pl.program_id(1) == 0)
    def _():
        scar[...] = jnp.zeros(scar.shape, F32)
        ccar[...] = jnp.zeros(ccar.shape, F32)

    dr = d_rwkv
    d = ma_out.shape[-1]
    h = _rms(x_ref[0], g_ref[...])
    hb = h.astype(BF16)

    ps = _dot(hb, win_s_ref[...])
    ma_out[0] = _sigmoid(_dot(hb, win_r_ref[:, 3 * dr:3 * dr + d]))

    prev, _ = _prev_rows(ps, scar, 0, ps.shape[1])
    sh = ps + (prev - ps) * mu_ref[...]
    r = sh[:, 0:dr]
    k = sh[:, dr:2 * dr]
    v = sh[:, 2 * dr:3 * dr]
    wa = sh[:, 3 * dr:3 * dr + LANE]
    gd = sh[:, 3 * dr + LANE:3 * dr + 2 * LANE]

    y = -(w0_ref[...] + _dot(jnp.tanh(wa).astype(BF16), du_ref[...]))
    softplus = jnp.maximum(y, 0.0) + jnp.log(1.0 + jnp.exp(-jnp.abs(y)))
    lw_out[0] = -jnp.exp(-softplus - 0.5)
    a = _sigmoid(a0_ref[...] + _dot(wa.astype(BF16), au_ref[...]))
    g_out[0] = _dot(_sigmoid(gd).astype(BF16), gu_ref[...])
    if has_vres:
        lo = _dot(hb, vd_ref[...])
        mix = _sigmoid(v0_ref[...] + _dot(lo.astype(BF16), vu_ref[...]))
        v = v + (vf_ref[0] - v) * mix
    kk = k * kk_ref[...]
    nrm = jnp.sqrt(_split_dot(kk * kk, ones_ref[...]))
    kk = kk / jnp.maximum(nrm, 1e-12)
    r_out[0] = r
    k_out[0] = k * (1.0 + (a - 1.0) * ka_ref[...])
    v_out[0] = v
    kkn_out[0] = kk
    kb_out[0] = kk * a

    mb_out[0] = _sigmoid(_dot(hb, win_r_ref[:, 3 * dr + d:3 * dr + 2 * d]))
    pc = _dot(hb, win_r_ref[:, 0:3 * dr])
    cu = pc[:, 2 * dr:3 * dr] * pc[:, 0:dr]
    c1, c2 = _prev_rows(cu, ccar, 0, dr)
    cw = cw_ref[...]
    yb_out[0] = pc[:, dr:2 * dr] * (cw[0:1, :] * c2 + cw[1:2, :] * c1 + cw[2:3, :] * cu)


def _const_spec(shape):
    nd = len(shape)
    return pl.BlockSpec(shape, lambda *_: (0,) * nd)


def _mix_in(x, norm_g, win_s, win_r, mu, w0, du, a0, au, gu, k_k, k_a, conv_w, ones_h, vres, tm):
    b, t, d = x.shape
    dr = w0.shape[-1]
    has_vres = vres is not None
    row = lambda w: pl.BlockSpec((1, tm, w), lambda i, j: (i, j, 0))
    consts = [norm_g, win_s, win_r, mu, w0, du, a0, au, gu, k_k, k_a, conv_w, ones_h]
    in_specs = [row(d)] + [_const_spec(c.shape) for c in consts]
    args = [x] + consts
    if has_vres:
        vd, vu, v0, v_first = vres
        in_specs += [_const_spec(vd.shape), _const_spec(vu.shape), _const_spec(v0.shape), row(dr)]
        args += [vd, vu, v0, v_first]
    out_shape = [jax.ShapeDtypeStruct((b, t, dr), F32)] * 8 + [jax.ShapeDtypeStruct((b, t, d), F32)] * 2
    out_specs = [row(dr)] * 8 + [row(d)] * 2
    return pl.pallas_call(
        functools.partial(_mix_in_kernel, has_vres, dr),
        grid=(b, t // tm),
        in_specs=in_specs,
        out_specs=out_specs,
        out_shape=out_shape,
        scratch_shapes=[pltpu.VMEM((CARRY, win_s.shape[1]), F32), pltpu.VMEM((CARRY, dr), F32)],
        compiler_params=pltpu.CompilerParams(dimension_semantics=("arbitrary", "arbitrary"),
                                             vmem_limit_bytes=VMEM_LIMIT),
        name="mix_in_vres" if has_vres else "mix_in",
    )(*args)


def _wkv_kernel(nc, npair, prep_group, emit_group, r_ref, lw_ref, k_ref, v_ref, kk_ref, kb_ref, g_ref, lnw_ref,
                lnb_ref, rk_ref, o_ref, g_s, gam_s, cm_s, pm_s, q_s):
    L = CHUNK
    row = lax.broadcasted_iota(jnp.int32, (L, PAIR), 0)
    col = lax.broadcasted_iota(jnp.int32, (L, PAIR), 1)
    pos = col & (HEAD_DIM - 1)
    strict = pos < row
    incl = pos <= row
    eye = (pos == row).astype(F32)
    head0 = col < HEAD_DIM
    r2 = lax.broadcasted_iota(jnp.int32, (PAIR, PAIR), 0)
    c2 = lax.broadcasted_iota(jnp.int32, (PAIR, PAIR), 1)
    bdmask = (r2 >= HEAD_DIM) == (c2 >= HEAD_DIM)
    ones_bd = bdmask.astype(BF16)
    tr = lax.broadcasted_iota(jnp.int32, (L, L), 0)
    tc = lax.broadcasted_iota(jnp.int32, (L, L), 1)
    tri = (tc <= tr).astype(BF16)

    dot = lambda a, b: _dot(a.astype(BF16), b.astype(BF16))
    dot_nt = lambda a, b: _dot_nt(a.astype(BF16), b.astype(BF16))
    dot_tn = lambda a, b: _dot_tn(a.astype(BF16), b.astype(BF16))

    def bd(x):
        return jnp.where(bdmask, jnp.concatenate([x, x], axis=0), jnp.zeros((), x.dtype))

    def packdiag(full):
        return jnp.where(head0, full[0:L, :], full[L:2 * L, :])

    def each(f, *lists):
        return [f(*xs) for xs in zip(*lists)]

    def rows(c):
        return pl.ds(pl.multiple_of(c * L, L), L)

    def lanes(p):
        return slice(p * PAIR, (p + 1) * PAIR)

    def prep(i, _):
        units = [(p, i * prep_group + j) for j in range(prep_group) for p in range(npair)]
        ld = lambda ref: [ref[0, rows(c), lanes(p)] for p, c in units]
        r, lw, k, v, kk, kb = ld(r_ref), ld(lw_ref), ld(k_ref), ld(v_ref), ld(kk_ref), ld(kb_ref)

        def cumsum(x):
            hi = x.astype(BF16)
            rem = x - hi.astype(F32)
            mid = rem.astype(BF16)
            low = (rem - mid.astype(F32)).astype(BF16)
            return _dot(tri, hi) + _dot(tri, mid) + _dot(tri, low)

        cs = each(cumsum, lw)
        cs_end = each(lambda x: x[L - 1:L, :], cs)
        kkd = each(lambda kk_, cs_, lw_: kk_ * jnp.exp(cs_ - lw_), kk, cs, lw)
        rd = each(lambda r_, cs_: r_ * jnp.exp(cs_), r, cs)
        p_inv = each(lambda cs_: jnp.exp(-cs_), cs)
        p_end = each(lambda ce, cs_: jnp.exp(ce - cs_), cs_end, cs)
        kh = each(lambda a, b: a * b, k, p_inv)
        bh = each(lambda a, b: a * b, kb, p_inv)
        ke = each(lambda a, b: a * b, k, p_end)
        be = each(lambda a, b: a * b, kb, p_end)

        def scores(kkd_, rd_, kh_, bh_):
            lhs = jnp.concatenate([kkd_, rd_], axis=0)
            rhs_t = jnp.concatenate([jnp.where(head0, kh_, 0.0), jnp.where(head0, 0.0, kh_),
                                     jnp.where(head0, bh_, 0.0), jnp.where(head0, 0.0, bh_)], axis=0)
            return dot_nt(lhs, rhs_t)

        aa = each(scores, kkd, rd, kh, bh)
        a1 = each(lambda x: jnp.where(strict, x[0:L, 0:PAIR], 0.0), aa)
        a2 = each(lambda x: jnp.where(strict, x[0:L, PAIR:2 * PAIR], 0.0), aa)
        a3 = each(lambda x: jnp.where(incl, x[L:2 * L, 0:PAIR], 0.0), aa)
        a4 = each(lambda x: jnp.where(incl, x[L:2 * L, PAIR:2 * PAIR], 0.0), aa)

        npow = each(lambda n: dot(n, bd(n)), a2)
        tm = each(lambda n: eye - n, a2)
        for _i in range(4):
            res = each(lambda t_, n: dot(jnp.concatenate([t_, n], axis=0), bd(n)), tm, npow)
            tm = each(lambda t_, x: t_ + x[0:L, :], tm, res)
            npow = each(lambda x: x[L:2 * L, :], res)
        tm = each(lambda t_, n: t_ + dot(t_, bd(n)), tm, npow)

        av = each(lambda x1, x3, v_: dot(jnp.concatenate([x1, x3], axis=0), bd(v_)), a1, a3, v)
        wu = each(lambda t_, kkd_, av_: dot(t_, jnp.concatenate([bd(kkd_), bd(av_[0:L, :])], axis=1)), tm, kkd, av)
        aw = each(lambda x4, wu_: dot(x4, jnp.concatenate([bd(wu_[:, 0:PAIR]), bd(wu_[:, PAIR:2 * PAIR])], axis=1)),
                  a4, wu)
        gm = each(lambda wu_, be_: dot_tn(wu_[:, 0:PAIR], be_), wu, be)
        cm = each(lambda v_, ke_, wu_, be_: dot_tn(v_, ke_) - dot_tn(wu_[:, PAIR:2 * PAIR], be_), v, ke, wu, be)
        for j, (p, c) in enumerate(units):
            pm_s[p, c] = (rd[j] - aw[j][:, 0:PAIR]).astype(BF16)
            q_s[p, c] = av[j][L:2 * L, :] - aw[j][:, PAIR:2 * PAIR]
            g_s[p, c] = jnp.where(bdmask, -gm[j], 0.0).astype(BF16)
            cm_s[p, c] = packdiag(cm[j])
            gam_s[p, c] = jnp.exp(cs_end[j])
        return 0

    lax.fori_loop(0, nc // prep_group, prep, 0)

    def emit(i, state):
        state = list(state)
        units = [(p, i * emit_group + j) for j in range(emit_group) for p in range(npair)]
        sb = []
        for p, c in units:
            s = state[p]
            sb.append(s.astype(BF16))
            state[p] = s * gam_s[p, c] + _dot(sb[-1], g_s[p, c]) + cm_s[p, c]
        o = [_dot_nt(pm_s[p, c], bd(sb_)) + q_s[p, c] for (p, c), sb_ in zip(units, sb)]
        rk = [r_ref[0, rows(c), lanes(p)] * k_ref[0, rows(c), lanes(p)] * rk_ref[:, lanes(p)] for p, c in units]
        sums = each(lambda o_, rk_: _dot(jnp.concatenate(_hi_lo(o_) + _hi_lo(rk_), axis=0), ones_bd), o, rk)
        dlt = each(lambda o_, s_: o_ - (s_[0:L, :] + s_[L:2 * L, :]) * (1.0 / HEAD_DIM), o, sums)
        var = each(lambda d: _split_dot(d * d, ones_bd) * (1.0 / HEAD_DIM), dlt)
        for (p, c), d, vr, s_ in zip(units, dlt, var, sums):
            on = d * lax.rsqrt(vr + GN_EPS) * lnw_ref[:, lanes(p)] + lnb_ref[:, lanes(p)]
            bonus = (s_[2 * L:3 * L, :] + s_[3 * L:4 * L, :]) * v_ref[0, rows(c), lanes(p)]
            o_ref[0, rows(c), lanes(p)] = (on + bonus) * g_ref[0, rows(c), lanes(p)]
        return tuple(state)

    lax.fori_loop(0, nc // emit_group, emit, tuple(jnp.zeros((L, PAIR), F32) for _ in range(npair)))


def _wkv(r, lw, k, v, kk, kb, g, ln_w, ln_b, r_k, npair, prep_group, emit_group):
    b, t, dr = r.shape
    nc = t // CHUNK
    wl = npair * PAIR
    seq = pl.BlockSpec((1, t, wl), lambda i, j: (i, 0, j))
    par = pl.BlockSpec((1, wl), lambda i, j: (0, j))
    return pl.pallas_call(
        functools.partial(_wkv_kernel, nc, npair, prep_group, emit_group),
        grid=(b, dr // wl),
        in_specs=[seq] * 7 + [par] * 3,
        out_specs=seq,
        out_shape=jax.ShapeDtypeStruct((b, t, dr), F32),
        scratch_shapes=[pltpu.VMEM((npair, nc, PAIR, PAIR), BF16),
                        pltpu.VMEM((npair, nc, 1, PAIR), F32),
                        pltpu.VMEM((npair, nc, CHUNK, PAIR), F32),
                        pltpu.VMEM((npair, nc, CHUNK, PAIR), BF16),
                        pltpu.VMEM((npair, nc, CHUNK, PAIR), F32)],
        compiler_params=pltpu.CompilerParams(dimension_semantics=("arbitrary", "arbitrary"),
                                             vmem_limit_bytes=VMEM_LIMIT),
        name="wkv",
    )(r, lw, k, v, kk, kb, g, ln_w, ln_b, r_k)


def _mix_out_kernel(n_sub, x_ref, ya_ref, yb_ref, ma_ref, mb_ref, pa_ref, pb_ref, wo_ref, o_ref):
    ts = x_ref.shape[0] // n_sub
    subs = [slice(i * ts, (i + 1) * ts) for i in range(n_sub)]
    pa = [_dot(ya_ref[s, :].astype(BF16), pa_ref[...]) for s in subs]
    pb = [_dot(yb_ref[s, :].astype(BF16), pb_ref[...]) for s in subs]
    for s, a, b in zip(subs, pa, pb):
        merged = ma_ref[s, :] * a + mb_ref[s, :] * b
        o_ref[s, :] = x_ref[s, :] + _dot(merged.astype(BF16), wo_ref[...])


def _mix_out(x, ya, yb, ma, mb, pa, pb, wo, tm, n_sub):
    n, d = x.shape
    dr = ya.shape[-1]
    row = lambda w: pl.BlockSpec((tm, w), lambda i: (i, 0))
    return pl.pallas_call(
        functools.partial(_mix_out_kernel, n_sub),
        grid=(n // tm,),
        in_specs=[row(d), row(dr), row(dr), row(d), row(d),
                  _const_spec(pa.shape), _const_spec(pb.shape), _const_spec(wo.shape)],
        out_specs=row(d),
        out_shape=jax.ShapeDtypeStruct((n, d), F32),
        compiler_params=pltpu.CompilerParams(dimension_semantics=("arbitrary",), vmem_limit_bytes=VMEM_LIMIT),
        name="mix_out",
    )(x, ya, yb, ma, mb, pa, pb, wo)


def _ffn_kernel(final_norm, cf, ahead, x_ref, g_ref, wu_ref, cw_ref, cb_ref, wd_ref, gf_ref, o_ref, zcar):
    @pl.when(pl.program_id(1) == 0)
    def _():
        zcar[...] = jnp.zeros(zcar.shape, F32)

    x = x_ref[0]
    hb = _rms(x, g_ref[...]).astype(BF16)
    f = wd_ref.shape[0]
    n_chunk = f // cf

    def up(c):
        return [(_dot(hb, wu_ref[:, lo:lo + cf]), lo) for lo in (c * cf, f + c * cf)]

    def conv(z, lo):
        z1, z2 = _prev_rows(z, zcar, lo, lo + cf)
        cw = cw_ref[:, lo:lo + cf]
        return cw[0:1, :] * z2 + cw[1:2, :] * z1 + cw[2:3, :] * z + cb_ref[:, lo:lo + cf]

    acc = x
    pending = [up(c) for c in range(min(ahead, n_chunk))]
    for c in range(n_chunk):
        if c + ahead < n_chunk:
            pending.append(up(c + ahead))
        zg, zu = [conv(z, lo) for z, lo in pending.pop(0)]
        act = zg * _sigmoid(zg) * zu
        acc = acc + _dot(act.astype(BF16), wd_ref[c * cf:(c + 1) * cf, :])
    if final_norm:
        acc = _rms(acc, gf_ref[...])
    o_ref[0] = acc


def _ffn(x, norm_g, w_up, conv_w, conv_b, w_down, final_g, final_norm, tm, cf, ahead):
    b, t, d = x.shape
    row = pl.BlockSpec((1, tm, d), lambda i, j: (i, j, 0))
    consts = [norm_g, w_up, conv_w, conv_b, w_down, final_g]
    return pl.pallas_call(
        functools.partial(_ffn_kernel, final_norm, cf, ahead),
        grid=(b, t // tm),
        in_specs=[row] + [_const_spec(c.shape) for c in consts],
        out_specs=row,
        out_shape=jax.ShapeDtypeStruct((b, t, d), F32),
        scratch_shapes=[pltpu.VMEM((CARRY, w_up.shape[1]), F32)],
        compiler_params=pltpu.CompilerParams(dimension_semantics=("arbitrary", "arbitrary"),
                                             vmem_limit_bytes=VMEM_LIMIT),
        name="ffn_final" if final_norm else "ffn",
    )(x, *consts)


def _pad_rows(w, rows, offset=0):
    out = jnp.zeros((rows, w.shape[1]), w.dtype)
    return out.at[offset:offset + w.shape[0]].set(w)


def _forward(x, norm_mix_g, w_in, mu_shift, w0, decay_up, a0, a_up, g_up, k_k, k_a, r_k, ln_x_w, ln_x_b,
             vres_down, vres_up, v0, conv_w, proj_a, proj_b, w_out, norm_ffn_g, w_up, ffn_conv_w, ffn_conv_b,
             w_down, norm_final_g, *, tm_mix, tm_out, out_sub, tm_ffn, ffn_cfg, wkv_cfg):
    b, t, d = x.shape
    depth = w_in.shape[0]
    dr = w0.shape[-1]
    shift_cols = mu_shift.shape[-1]
    head = jnp.arange(dr) // HEAD_DIM
    ones_h = (head[:, None] == head[None, :]).astype(BF16)
    row = lambda p: p.reshape(1, -1)
    v_first = None
    for l in range(depth):
        win = w_in[l].astype(BF16)
        du = _pad_rows(decay_up[l], LANE).astype(BF16)
        au = _pad_rows(a_up[l], LANE, DECAY_LORA).astype(BF16)
        vres = None
        if l > 0:
            vd = jnp.pad(vres_down[l - 1], ((0, 0), (0, LANE - VRES_LORA))).astype(BF16)
            vu = _pad_rows(vres_up[l - 1], LANE).astype(BF16)
            vres = (vd, vu, row(v0[l - 1]), v_first)
        r, lw, k, v, kk, kb, g, yb, ma, mb = _mix_in(
            x, row(norm_mix_g[l]), win[:, :shift_cols], win[:, shift_cols:], row(mu_shift[l]), row(w0[l]), du,
            row(a0[l]), au, g_up[l].astype(BF16), row(k_k[l]), row(k_a[l]), conv_w[l], ones_h, vres, tm_mix)
        if l == 0:
            v_first = v
        ya = _wkv(r, lw, k, v, kk, kb, g, row(ln_x_w[l]), row(ln_x_b[l]), row(r_k[l]), *wkv_cfg)
        x = _mix_out(x.reshape(b * t, d), ya.reshape(b * t, dr), yb.reshape(b * t, dr), ma.reshape(b * t, d),
                     mb.reshape(b * t, d), proj_a[l].astype(BF16), proj_b[l].astype(BF16),
                     w_out[l].astype(BF16), tm_out, out_sub).reshape(b, t, d)
        x = _ffn(x, row(norm_ffn_g[l]), w_up[l].astype(BF16), ffn_conv_w[l], row(ffn_conv_b[l]),
                 w_down[l].astype(BF16), row(norm_final_g), l == depth - 1, tm_ffn, *ffn_cfg)
    return x


def kernel(x, norm_mix_g, w_in, mu_shift, w0, decay_up, a0, a_up, g_up, k_k, k_a, r_k, ln_x_w, ln_x_b, vres_down, vres_up, v0, conv_w, proj_a, proj_b, w_out, norm_ffn_g, w_up, ffn_conv_w, ffn_conv_b, w_down, norm_final_g):
    return _forward(x, norm_mix_g, w_in, mu_shift, w0, decay_up, a0, a_up, g_up, k_k, k_a, r_k, ln_x_w, ln_x_b,
                    vres_down, vres_up, v0, conv_w, proj_a, proj_b, w_out, norm_ffn_g, w_up, ffn_conv_w,
                    ffn_conv_b, w_down, norm_final_g,
                    tm_mix=256, tm_out=512, out_sub=2, tm_ffn=256, ffn_cfg=(256, 3), wkv_cfg=(2, 8, 4))
```

```python
import functools

import jax
import jax.numpy as jnp
from jax import lax
from jax.experimental import pallas as pl
from jax.experimental.pallas import tpu as pltpu

F32 = jnp.float32
BF16 = jnp.bfloat16

HEAD_DIM = 64
CHUNK = 64
PAIR = 2 * HEAD_DIM
DECAY_LORA = 64
ICLR_LORA = 64
GATE_LORA = 128
VRES_LORA = 32
LANE = 128
CARRY = 8
RMS_EPS = 1e-6
GN_EPS = HEAD_DIM * 1e-5
VMEM_LIMIT = 56 * 1024 * 1024


def _dot(a, b):
    return jnp.dot(a, b, preferred_element_type=F32)


def _dot_nt(a, b):
    return lax.dot_general(a, b, (((1,), (1,)), ((), ())), preferred_element_type=F32)


def _dot_tn(a, b):
    return lax.dot_general(a, b, (((0,), (0,)), ((), ())), preferred_element_type=F32)


def _hi_lo(x):
    hi = x.astype(BF16)
    return hi, (x - hi.astype(F32)).astype(BF16)


def _rms(x, g):
    return x * lax.rsqrt(jnp.mean(x * x, axis=-1, keepdims=True) + RMS_EPS) * g


def _sigmoid(x):
    return 1.0 / (1.0 + jnp.exp(-x))


def _prev_rows(z, carry_ref, lo, hi):
    tm = z.shape[0]
    prev = carry_ref[:, lo:hi]
    carry_ref[:, lo:hi] = z[tm - CARRY:tm, :]
    row = lax.broadcasted_iota(jnp.int32, prev.shape, 0)
    out = []
    for shift in (1, 2):
        rolled = pltpu.roll(z, shift, 0)
        first = jnp.where(row < shift, pltpu.roll(prev, shift, 0), rolled[0:CARRY, :])
        out.append(jnp.concatenate([first, rolled[CARRY:, :]], axis=0))
    return out


def _mix_in_kernel(has_vres, d_rwkv, *refs):
    if has_vres:
        (x_ref, g_ref, win_s_ref, win_r_ref, mu_ref, w0_ref, du_ref, a0_ref, au_ref, gu_ref, kk_ref, ka_ref,
         cw_ref, ones_ref, vd_ref, vu_ref, v0_ref, vf_ref,
         r_out, lw_out, k_out, v_out, kkn_out, kb_out, g_out, yb_out, ma_out, mb_out, scar, ccar) = refs
    else:
        (x_ref, g_ref, win_s_ref, win_r_ref, mu_ref, w0_ref, du_ref, a0_ref, au_ref, gu_ref, kk_ref, ka_ref,
         cw_ref, ones_ref,
         r_out, lw_out, k_out, v_out, kkn_out, kb_out, g_out, yb_out, ma_out, mb_out, scar, ccar) = refs

    @pl.when(pl.program_id(1) == 0)
    def _():
        scar[...] = jnp.zeros(scar.shape, F32)
        ccar[...] = jnp.zeros(ccar.shape, F32)

    dr = d_rwkv
    d = ma_out.shape[-1]
    h = _rms(x_ref[0], g_ref[...])
    hb = h.astype(BF16)

    ps = _dot(hb, win_s_ref[...])
    ma_out[0] = _sigmoid(_dot(hb, win_r_ref[:, 3 * dr:3 * dr + d])).astype(ma_out.dtype)

    prev, _ = _prev_rows(ps, scar, 0, ps.shape[1])
    sh = ps + (prev - ps) * mu_ref[...]
    r = sh[:, 0:dr]
    k = sh[:, dr:2 * dr]
    v = sh[:, 2 * dr:3 * dr]
    wa = sh[:, 3 * dr:3 * dr + LANE]
    gd = sh[:, 3 * dr + LANE:3 * dr + 2 * LANE]

    y = -(w0_ref[...] + _dot(jnp.tanh(wa).astype(BF16), du_ref[...]))
    softplus = jnp.maximum(y, 0.0) + jnp.log(1.0 + jnp.exp(-jnp.abs(y)))
    lw_out[0] = -jnp.exp(-softplus - 0.5)
    a = _sigmoid(a0_ref[...] + _dot(wa.astype(BF16), au_ref[...]))
    g_out[0] = _dot(_sigmoid(gd).astype(BF16), gu_ref[...]).astype(g_out.dtype)
    if has_vres:
        lo = _dot(hb, vd_ref[...])
        mix = _sigmoid(v0_ref[...] + _dot(lo.astype(BF16), vu_ref[...]))
        v = v + (vf_ref[0] - v) * mix
    kk = k * kk_ref[...]
    nrm = jnp.sqrt(_dot((kk * kk).astype(BF16), ones_ref[...]))
    kk = kk / jnp.maximum(nrm, 1e-12)
    r_out[0] = r
    k_out[0] = k * (1.0 + (a - 1.0) * ka_ref[...])
    v_out[0] = v
    kkn_out[0] = kk
    kb_out[0] = kk * a

    mb_out[0] = _sigmoid(_dot(hb, win_r_ref[:, 3 * dr + d:3 * dr + 2 * d])).astype(mb_out.dtype)
    pc = _dot(hb, win_r_ref[:, 0:3 * dr])
    cu = pc[:, 2 * dr:3 * dr] * pc[:, 0:dr]
    c1, c2 = _prev_rows(cu, ccar, 0, dr)
    cw = cw_ref[...]
    yb_out[0] = (pc[:, dr:2 * dr] * (cw[0:1, :] * c2 + cw[1:2, :] * c1 + cw[2:3, :] * cu)).astype(yb_out.dtype)


def _const_spec(shape):
    nd = len(shape)
    return pl.BlockSpec(shape, lambda *_: (0,) * nd, pipeline_mode=pl.Buffered(1))


def _mix_in(x, norm_g, win_s, win_r, mu, w0, du, a0, au, gu, k_k, k_a, conv_w, ones_h, vres, tm):
    b, t, d = x.shape
    dr = w0.shape[-1]
    has_vres = vres is not None
    row = lambda w: pl.BlockSpec((1, tm, w), lambda i, j: (i, j, 0))
    consts = [norm_g, win_s, win_r, mu, w0, du, a0, au, gu, k_k, k_a, conv_w, ones_h]
    in_specs = [row(d)] + [_const_spec(c.shape) for c in consts]
    args = [x] + consts
    if has_vres:
        vd, vu, v0, v_first = vres
        in_specs += [_const_spec(vd.shape), _const_spec(vu.shape), _const_spec(v0.shape), row(dr)]
        args += [vd, vu, v0, v_first]
    out_shape = ([jax.ShapeDtypeStruct((b, t, dr), F32)] * 6 + [jax.ShapeDtypeStruct((b, t, dr), BF16)] * 2
                 + [jax.ShapeDtypeStruct((b, t, d), BF16)] * 2)
    out_specs = [row(dr)] * 8 + [row(d)] * 2
    return pl.pallas_call(
        functools.partial(_mix_in_kernel, has_vres, dr),
        grid=(b, t // tm),
        in_specs=in_specs,
        out_specs=out_specs,
        out_shape=out_shape,
        scratch_shapes=[pltpu.VMEM((CARRY, win_s.shape[1]), F32), pltpu.VMEM((CARRY, dr), F32)],
        compiler_params=pltpu.CompilerParams(dimension_semantics=("arbitrary", "arbitrary"),
                                             vmem_limit_bytes=VMEM_LIMIT),
        name="mix_in_vres" if has_vres else "mix_in",
    )(*args)


def _wkv_kernel(nc, npair, group, r_ref, lw_ref, k_ref, v_ref, kk_ref, kb_ref, g_ref, lnw_ref, lnb_ref, rk_ref,
                o_ref):
    L = CHUNK
    row = lax.broadcasted_iota(jnp.int32, (L, PAIR), 0)
    col = lax.broadcasted_iota(jnp.int32, (L, PAIR), 1)
    pos = col & (HEAD_DIM - 1)
    strict = pos < row
    incl = pos <= row
    eye = (pos == row).astype(F32)
    head0 = col < HEAD_DIM
    r2 = lax.broadcasted_iota(jnp.int32, (PAIR, PAIR), 0)
    c2 = lax.broadcasted_iota(jnp.int32, (PAIR, PAIR), 1)
    bdmask = (r2 >= HEAD_DIM) == (c2 >= HEAD_DIM)
    ones_bd = bdmask.astype(BF16)
    tr = lax.broadcasted_iota(jnp.int32, (L, L), 0)
    tc = lax.broadcasted_iota(jnp.int32, (L, L), 1)
    tri = (tc <= tr).astype(BF16)

    dot = lambda a, b: _dot(a.astype(BF16), b.astype(BF16))
    dot_nt = lambda a, b: _dot_nt(a.astype(BF16), b.astype(BF16))
    dot_tn = lambda a, b: _dot_tn(a.astype(BF16), b.astype(BF16))

    def bd(x):
        return jnp.where(bdmask, jnp.concatenate([x, x], axis=0), jnp.zeros((), x.dtype))

    def packdiag(full):
        return jnp.where(head0, full[0:L, :], full[L:2 * L, :])

    def each(f, *lists):
        return [f(*xs) for xs in zip(*lists)]

    def rows(c):
        return slice(c * L, (c + 1) * L)

    def lanes(p):
        return slice(p * PAIR, (p + 1) * PAIR)

    def prep(units, out):
        ld = lambda ref: [ref[0, rows(c), lanes(p)] for p, c in units]
        r, lw, k, v, kk, kb = ld(r_ref), ld(lw_ref), ld(k_ref), ld(v_ref), ld(kk_ref), ld(kb_ref)

        cs = each(lambda x: _dot(tri, jnp.concatenate(_hi_lo(x), axis=1)), lw)
        cs = each(lambda x: x[:, 0:PAIR] + x[:, PAIR:2 * PAIR], cs)
        yield
        cs_end = each(lambda x: x[L - 1:L, :], cs)
        kkd = each(lambda kk_, cs_, lw_: kk_ * jnp.exp(cs_ - lw_), kk, cs, lw)
        rd = each(lambda r_, cs_: r_ * jnp.exp(cs_), r, cs)
        p_inv = each(lambda cs_: jnp.exp(-cs_), cs)
        p_end = each(lambda ce, cs_: jnp.exp(ce - cs_), cs_end, cs)
        kh = each(lambda a, b: a * b, k, p_inv)
        bh = each(lambda a, b: a * b, kb, p_inv)
        ke = each(lambda a, b: a * b, k, p_end)
        be = each(lambda a, b: a * b, kb, p_end)

        def scores(kkd_, rd_, kh_, bh_):
            lhs = jnp.concatenate([kkd_, rd_], axis=0)
            rhs_t = jnp.concatenate([jnp.where(head0, kh_, 0.0), jnp.where(head0, 0.0, kh_),
                                     jnp.where(head0, bh_, 0.0), jnp.where(head0, 0.0, bh_)], axis=0)
            return dot_nt(lhs, rhs_t)

        aa = each(scores, kkd, rd, kh, bh)
        yield
        a1 = each(lambda x: jnp.where(strict, x[0:L, 0:PAIR], 0.0), aa)
        a2 = each(lambda x: jnp.where(strict, x[0:L, PAIR:2 * PAIR], 0.0), aa)
        a3 = each(lambda x: jnp.where(incl, x[L:2 * L, 0:PAIR], 0.0), aa)
        a4 = each(lambda x: jnp.where(incl, x[L:2 * L, PAIR:2 * PAIR], 0.0), aa)

        npow = each(lambda n: dot(n, bd(n)), a2)
        tm = each(lambda n: eye - n, a2)
        av = each(lambda x1, x3, v_: dot(jnp.concatenate([x1, x3], axis=0), bd(v_)), a1, a3, v)
        yield
        for _i in range(4):
            res = each(lambda t_, n: dot(jnp.concatenate([t_, n], axis=0), bd(n)), tm, npow)
            tm = each(lambda t_, x: t_ + x[0:L, :], tm, res)
            npow = each(lambda x: x[L:2 * L, :], res)
            yield
        tm = each(lambda t_, n: t_ + dot(t_, bd(n)), tm, npow)
        yield
        wu = each(lambda t_, kkd_, av_: dot(t_, jnp.concatenate([bd(kkd_), bd(av_[0:L, :])], axis=1)), tm, kkd, av)
        yield
        aw = each(lambda x4, wu_: dot(x4, jnp.concatenate([bd(wu_[:, 0:PAIR]), bd(wu_[:, PAIR:2 * PAIR])], axis=1)),
                  a4, wu)
        gm = each(lambda wu_, be_: dot_tn(wu_[:, 0:PAIR], be_), wu, be)
        yield
        cm = each(lambda v_, ke_, wu_, be_: dot_tn(jnp.concatenate([v_, wu_[:, PAIR:2 * PAIR]], axis=0),
                                                   jnp.concatenate([ke_, -be_], axis=0)), v, ke, wu, be)
        out["pm"] = each(lambda rd_, aw_: (rd_ - aw_[:, 0:PAIR]).astype(BF16), rd, aw)
        out["q"] = each(lambda av_, aw_: av_[L:2 * L, :] - aw_[:, PAIR:2 * PAIR], av, aw)
        out["g"] = each(lambda gm_: jnp.where(bdmask, -gm_, 0.0).astype(BF16), gm)
        out["cm"] = each(packdiag, cm)
        out["gam"] = each(jnp.exp, cs_end)
        yield

    def tail(units, fac, state):
        sb = []
        for j, (p, c) in enumerate(units):
            s = state[p]
            sb.append(s.astype(BF16))
            state[p] = s * fac["gam"][j] + _dot(sb[-1], fac["g"][j]) + fac["cm"][j]
            if p == npair - 1:
                yield
        o = [_dot_nt(pm_, bd(sb_)) + q_ for pm_, sb_, q_ in zip(fac["pm"], sb, fac["q"])]
        rk = [r_ref[0, rows(c), lanes(p)] * k_ref[0, rows(c), lanes(p)] * rk_ref[:, lanes(p)] for p, c in units]
        yield
        sums = each(lambda o_, rk_: _dot(jnp.concatenate([o_.astype(BF16), rk_.astype(BF16)], axis=0), ones_bd),
                    o, rk)
        yield
        dlt = each(lambda o_, s_: o_ - s_[0:L, :] * (1.0 / HEAD_DIM), o, sums)
        var = each(lambda d: _dot((d * d).astype(BF16), ones_bd) * (1.0 / HEAD_DIM), dlt)
        yield
        for (p, c), d, vr, s_ in zip(units, dlt, var, sums):
            on = d * lax.rsqrt(vr + GN_EPS) * lnw_ref[:, lanes(p)] + lnb_ref[:, lanes(p)]
            bonus = s_[L:2 * L, :] * v_ref[0, rows(c), lanes(p)]
            o_ref[0, rows(c), lanes(p)] = ((on + bonus) * g_ref[0, rows(c), lanes(p)]).astype(o_ref.dtype)
        yield

    def run(*gens):
        gens = list(gens)
        while gens:
            for gen in list(gens):
                try:
                    next(gen)
                except StopIteration:
                    gens.remove(gen)

    n_group = nc // group
    units = [[(p, gi * group + j) for j in range(group) for p in range(npair)] for gi in range(n_group)]
    state = [jnp.zeros((L, PAIR), F32) for _ in range(npair)]
    fac = {}
    run(prep(units[0], fac))
    for gi in range(n_group):
        nxt = {}
        if gi + 1 < n_group:
            run(tail(units[gi], fac, state), prep(units[gi + 1], nxt))
        else:
            run(tail(units[gi], fac, state))
        fac = nxt


def _wkv(r, lw, k, v, kk, kb, g, ln_w, ln_b, r_k, npair, group):
    b, t, dr = r.shape
    nc = t // CHUNK
    wl = npair * PAIR
    seq = pl.BlockSpec((1, t, wl), lambda i, j: (i, 0, j))
    par = pl.BlockSpec((1, wl), lambda i, j: (0, j))
    return pl.pallas_call(
        functools.partial(_wkv_kernel, nc, npair, group),
        grid=(b, dr // wl),
        in_specs=[seq] * 7 + [par] * 3,
        out_specs=seq,
        out_shape=jax.ShapeDtypeStruct((b, t, dr), BF16),
        compiler_params=pltpu.CompilerParams(dimension_semantics=("arbitrary", "arbitrary"),
                                             vmem_limit_bytes=VMEM_LIMIT),
        name="wkv",
    )(r, lw, k, v, kk, kb, g, ln_w, ln_b, r_k)


def _mix_out_kernel(n_sub, x_ref, ya_ref, yb_ref, ma_ref, mb_ref, pa_ref, pb_ref, wo_ref, o_ref):
    ts = x_ref.shape[0] // n_sub
    subs = [slice(i * ts, (i + 1) * ts) for i in range(n_sub)]
    pa = [_dot(ya_ref[s, :], pa_ref[...]) for s in subs]
    pb = [_dot(yb_ref[s, :], pb_ref[...]) for s in subs]
    for s, a, b in zip(subs, pa, pb):
        merged = ma_ref[s, :].astype(F32) * a + mb_ref[s, :].astype(F32) * b
        o_ref[s, :] = x_ref[s, :] + _dot(merged.astype(BF16), wo_ref[...])


def _mix_out(x, ya, yb, ma, mb, pa, pb, wo, tm, n_sub):
    n, d = x.shape
    dr = ya.shape[-1]
    row = lambda w: pl.BlockSpec((tm, w), lambda i: (i, 0))
    return pl.pallas_call(
        functools.partial(_mix_out_kernel, n_sub),
        grid=(n // tm,),
        in_specs=[row(d), row(dr), row(dr), row(d), row(d),
                  _const_spec(pa.shape), _const_spec(pb.shape), _const_spec(wo.shape)],
        out_specs=row(d),
        out_shape=jax.ShapeDtypeStruct((n, d), F32),
        compiler_params=pltpu.CompilerParams(dimension_semantics=("arbitrary",), vmem_limit_bytes=VMEM_LIMIT),
        name="mix_out",
    )(x, ya, yb, ma, mb, pa, pb, wo)


def _ffn_kernel(final_norm, cf, ahead, x_ref, g_ref, wu_ref, cw_ref, cb_ref, wd_ref, gf_ref, o_ref, zcar):
    @pl.when(pl.program_id(1) == 0)
    def _():
        zcar[...] = jnp.zeros(zcar.shape, F32)

    x = x_ref[0]
    hb = _rms(x, g_ref[...]).astype(BF16)
    f = wd_ref.shape[0]
    n_chunk = f // cf

    def up(c):
        return [(_dot(hb, wu_ref[:, lo:lo + cf]), lo) for lo in (c * cf, f + c * cf)]

    def conv(z, lo):
        z1, z2 = _prev_rows(z, zcar, lo, lo + cf)
        cw = cw_ref[:, lo:lo + cf]
        return cw[0:1, :] * z2 + cw[1:2, :] * z1 + cw[2:3, :] * z + cb_ref[:, lo:lo + cf]

    acc = x
    pending = [up(c) for c in range(min(ahead, n_chunk))]
    for c in range(n_chunk):
        if c + ahead < n_chunk:
            pending.append(up(c + ahead))
        zg, zu = [conv(z, lo) for z, lo in pending.pop(0)]
        act = zg * _sigmoid(zg) * zu
        acc = acc + _dot(act.astype(BF16), wd_ref[c * cf:(c + 1) * cf, :])
    if final_norm:
        acc = _rms(acc, gf_ref[...])
    o_ref[0] = acc


def _ffn(x, norm_g, w_up, conv_w, conv_b, w_down, final_g, final_norm, tm, cf, ahead):
    b, t, d = x.shape
    row = pl.BlockSpec((1, tm, d), lambda i, j: (i, j, 0))
    consts = [norm_g, w_up, conv_w, conv_b, w_down, final_g]
    return pl.pallas_call(
        functools.partial(_ffn_kernel, final_norm, cf, ahead),
        grid=(b, t // tm),
        in_specs=[row] + [_const_spec(c.shape) for c in consts],
        out_specs=row,
        out_shape=jax.ShapeDtypeStruct((b, t, d), F32),
        scratch_shapes=[pltpu.VMEM((CARRY, w_up.shape[1]), F32)],
        compiler_params=pltpu.CompilerParams(dimension_semantics=("arbitrary", "arbitrary"),
                                             vmem_limit_bytes=VMEM_LIMIT),
        name="ffn_final" if final_norm else "ffn",
    )(x, *consts)


def _pad_rows(w, rows, offset=0):
    out = jnp.zeros((rows, w.shape[1]), w.dtype)
    return out.at[offset:offset + w.shape[0]].set(w)


def _forward(x, norm_mix_g, w_in, mu_shift, w0, decay_up, a0, a_up, g_up, k_k, k_a, r_k, ln_x_w, ln_x_b,
             vres_down, vres_up, v0, conv_w, proj_a, proj_b, w_out, norm_ffn_g, w_up, ffn_conv_w, ffn_conv_b,
             w_down, norm_final_g, *, tm_mix, tm_out, out_sub, tm_ffn, ffn_cfg, wkv_cfg):
    b, t, d = x.shape
    depth = w_in.shape[0]
    dr = w0.shape[-1]
    shift_cols = mu_shift.shape[-1]
    head = jnp.arange(dr) // HEAD_DIM
    ones_h = (head[:, None] == head[None, :]).astype(BF16)
    row = lambda p: p.reshape(1, -1)
    v_first = None
    for l in range(depth):
        win = w_in[l].astype(BF16)
        du = _pad_rows(decay_up[l], LANE).astype(BF16)
        au = _pad_rows(a_up[l], LANE, DECAY_LORA).astype(BF16)
        vres = None
        if l > 0:
            vd = jnp.pad(vres_down[l - 1], ((0, 0), (0, LANE - VRES_LORA))).astype(BF16)
            vu = _pad_rows(vres_up[l - 1], LANE).astype(BF16)
            vres = (vd, vu, row(v0[l - 1]), v_first)
        r, lw, k, v, kk, kb, g, yb, ma, mb = _mix_in(
            x, row(norm_mix_g[l]), win[:, :shift_cols], win[:, shift_cols:], row(mu_shift[l]), row(w0[l]), du,
            row(a0[l]), au, g_up[l].astype(BF16), row(k_k[l]), row(k_a[l]), conv_w[l], ones_h, vres, tm_mix)
        if l == 0:
            v_first = v
        ya = _wkv(r, lw, k, v, kk, kb, g, row(ln_x_w[l]), row(ln_x_b[l]), row(r_k[l]), *wkv_cfg)
        x = _mix_out(x.reshape(b * t, d), ya.reshape(b * t, dr), yb.reshape(b * t, dr), ma.reshape(b * t, d),
                     mb.reshape(b * t, d), proj_a[l].astype(BF16), proj_b[l].astype(BF16),
                     w_out[l].astype(BF16), tm_out, out_sub).reshape(b, t, d)
        x = _ffn(x, row(norm_ffn_g[l]), w_up[l].astype(BF16), ffn_conv_w[l], row(ffn_conv_b[l]),
                 w_down[l].astype(BF16), row(norm_final_g), l == depth - 1, tm_ffn, *ffn_cfg)
    return x


def kernel(x, norm_mix_g, w_in, mu_shift, w0, decay_up, a0, a_up, g_up, k_k, k_a, r_k, ln_x_w, ln_x_b, vres_down, vres_up, v0, conv_w, proj_a, proj_b, w_out, norm_ffn_g, w_up, ffn_conv_w, ffn_conv_b, w_down, norm_final_g):
    return _forward(x, norm_mix_g, w_in, mu_shift, w0, decay_up, a0, a_up, g_up, k_k, k_a, r_k, ln_x_w, ln_x_b,
                    vres_down, vres_up, v0, conv_w, proj_a, proj_b, w_out, norm_ffn_g, w_up, ffn_conv_w,
                    ffn_conv_b, w_down, norm_final_g,
                    tm_mix=256, tm_out=512, out_sub=2, tm_ffn=256, ffn_cfg=(256, 3), wkv_cfg=(2, 8))
```

```python
import functools

import jax
import jax.numpy as jnp
from jax import lax
from jax.experimental import pallas as pl
from jax.experimental.pallas import tpu as pltpu

F32 = jnp.float32
BF16 = jnp.bfloat16

HEAD_DIM = 64
CHUNK = 64
PAIR = 2 * HEAD_DIM
DECAY_LORA = 64
ICLR_LORA = 64
GATE_LORA = 128
VRES_LORA = 32
LANE = 128
CARRY = 8
RMS_EPS = 1e-6
GN_EPS = HEAD_DIM * 1e-5
VMEM_LIMIT = 56 * 1024 * 1024


def _dot(a, b):
    return jnp.dot(a, b, preferred_element_type=F32)


def _dot_nt(a, b):
    return lax.dot_general(a, b, (((1,), (1,)), ((), ())), preferred_element_type=F32)


def _dot_tn(a, b):
    return lax.dot_general(a, b, (((0,), (0,)), ((), ())), preferred_element_type=F32)


def _hi_lo(x):
    hi = x.astype(BF16)
    return hi, (x - hi.astype(F32)).astype(BF16)


def _rms(x, g):
    return x * lax.rsqrt(jnp.mean(x * x, axis=-1, keepdims=True) + RMS_EPS) * g


def _sigmoid(x):
    return 1.0 / (1.0 + jnp.exp(-x))


def _prev_rows(z, carry_ref, lo, hi):
    tm = z.shape[0]
    prev = carry_ref[:, lo:hi]
    carry_ref[:, lo:hi] = z[tm - CARRY:tm, :]
    row = lax.broadcasted_iota(jnp.int32, prev.shape, 0)
    out = []
    for shift in (1, 2):
        rolled = pltpu.roll(z, shift, 0)
        first = jnp.where(row < shift, pltpu.roll(prev, shift, 0), rolled[0:CARRY, :])
        out.append(jnp.concatenate([first, rolled[CARRY:, :]], axis=0))
    return out


def _mix_in_kernel(has_vres, d_rwkv, n_sub, *refs):
    if has_vres:
        (x_ref, g_ref, win_s_ref, win_r_ref, mu_ref, w0_ref, du_ref, a0_ref, au_ref, gu_ref, kk_ref, ka_ref,
         cw_ref, ones_ref, vd_ref, vu_ref, v0_ref, vf_ref,
         r_out, lw_out, k_out, v_out, kkn_out, kb_out, g_out, yb_out, ma_out, mb_out, scar, ccar) = refs
    else:
        (x_ref, g_ref, win_s_ref, win_r_ref, mu_ref, w0_ref, du_ref, a0_ref, au_ref, gu_ref, kk_ref, ka_ref,
         cw_ref, ones_ref,
         r_out, lw_out, k_out, v_out, kkn_out, kb_out, g_out, yb_out, ma_out, mb_out, scar, ccar) = refs

    @pl.when(pl.program_id(1) == 0)
    def _():
        scar[...] = jnp.zeros(scar.shape, F32)
        ccar[...] = jnp.zeros(ccar.shape, F32)

    dr = d_rwkv
    d = ma_out.shape[-1]
    ts = x_ref.shape[1] // n_sub
    subs = [slice(i * ts, (i + 1) * ts) for i in range(n_sub)]
    hbs = [_rms(x_ref[0, s, :], g_ref[...]).astype(BF16) for s in subs]

    for s, hb in zip(subs, hbs):
        ps = _dot(hb, win_s_ref[...])
        ma_out[0, s, :] = _sigmoid(_dot(hb, win_r_ref[:, 3 * dr:3 * dr + d])).astype(ma_out.dtype)

        prev, _ = _prev_rows(ps, scar, 0, ps.shape[1])
        sh = ps + (prev - ps) * mu_ref[...]
        r = sh[:, 0:dr]
        k = sh[:, dr:2 * dr]
        v = sh[:, 2 * dr:3 * dr]
        wa = sh[:, 3 * dr:3 * dr + LANE]
        gd = sh[:, 3 * dr + LANE:3 * dr + 2 * LANE]

        y = -(w0_ref[...] + _dot(jnp.tanh(wa).astype(BF16), du_ref[...]))
        softplus = jnp.maximum(y, 0.0) + jnp.log(1.0 + jnp.exp(-jnp.abs(y)))
        lw_out[0, s, :] = -jnp.exp(-softplus - 0.5)
        a = _sigmoid(a0_ref[...] + _dot(wa.astype(BF16), au_ref[...]))
        g_out[0, s, :] = _dot(_sigmoid(gd).astype(BF16), gu_ref[...]).astype(g_out.dtype)
        if has_vres:
            lo = _dot(hb, vd_ref[...])
            mix = _sigmoid(v0_ref[...] + _dot(lo.astype(BF16), vu_ref[...]))
            v = v + (vf_ref[0, s, :] - v) * mix
        kk = k * kk_ref[...]
        nrm = jnp.sqrt(_dot((kk * kk).astype(BF16), ones_ref[...]))
        kk = kk / jnp.maximum(nrm, 1e-12)
        r_out[0, s, :] = r
        k_out[0, s, :] = k * (1.0 + (a - 1.0) * ka_ref[...])
        v_out[0, s, :] = v
        kkn_out[0, s, :] = kk
        kb_out[0, s, :] = kk * a

        mb_out[0, s, :] = _sigmoid(_dot(hb, win_r_ref[:, 3 * dr + d:3 * dr + 2 * d])).astype(mb_out.dtype)
        pc = _dot(hb, win_r_ref[:, 0:3 * dr])
        cu = pc[:, 2 * dr:3 * dr] * pc[:, 0:dr]
        c1, c2 = _prev_rows(cu, ccar, 0, dr)
        cw = cw_ref[...]
        yb_out[0, s, :] = (pc[:, dr:2 * dr] * (cw[0:1, :] * c2 + cw[1:2, :] * c1 + cw[2:3, :] * cu)
                           ).astype(yb_out.dtype)


def _const_spec(shape):
    nd = len(shape)
    return pl.BlockSpec(shape, lambda *_: (0,) * nd, pipeline_mode=pl.Buffered(1))


def _mix_in(x, norm_g, win_s, win_r, mu, w0, du, a0, au, gu, k_k, k_a, conv_w, ones_h, vres, tm, n_sub):
    b, t, d = x.shape
    dr = w0.shape[-1]
    has_vres = vres is not None
    row = lambda w: pl.BlockSpec((1, tm, w), lambda i, j: (i, j, 0))
    consts = [norm_g, win_s, win_r, mu, w0, du, a0, au, gu, k_k, k_a, conv_w, ones_h]
    in_specs = [row(d)] + [_const_spec(c.shape) for c in consts]
    args = [x] + consts
    if has_vres:
        vd, vu, v0, v_first = vres
        in_specs += [_const_spec(vd.shape), _const_spec(vu.shape), _const_spec(v0.shape), row(dr)]
        args += [vd, vu, v0, v_first]
    out_shape = ([jax.ShapeDtypeStruct((b, t, dr), F32)] * 6 + [jax.ShapeDtypeStruct((b, t, dr), BF16)] * 2
                 + [jax.ShapeDtypeStruct((b, t, d), BF16)] * 2)
    out_specs = [row(dr)] * 8 + [row(d)] * 2
    return pl.pallas_call(
        functools.partial(_mix_in_kernel, has_vres, dr, n_sub),
        grid=(b, t // tm),
        in_specs=in_specs,
        out_specs=out_specs,
        out_shape=out_shape,
        scratch_shapes=[pltpu.VMEM((CARRY, win_s.shape[1]), F32), pltpu.VMEM((CARRY, dr), F32)],
        compiler_params=pltpu.CompilerParams(dimension_semantics=("arbitrary", "arbitrary"),
                                             vmem_limit_bytes=VMEM_LIMIT),
        name="mix_in_vres" if has_vres else "mix_in",
    )(*args)


def _wkv_kernel(nc, npair, group, r_ref, lw_ref, k_ref, v_ref, kk_ref, kb_ref, g_ref, lnw_ref, lnb_ref, rk_ref,
                o_ref):
    L = CHUNK
    row = lax.broadcasted_iota(jnp.int32, (L, PAIR), 0)
    col = lax.broadcasted_iota(jnp.int32, (L, PAIR), 1)
    pos = col & (HEAD_DIM - 1)
    strict = pos < row
    incl = pos <= row
    eye = (pos == row).astype(F32)
    head0 = col < HEAD_DIM
    r2 = lax.broadcasted_iota(jnp.int32, (PAIR, PAIR), 0)
    c2 = lax.broadcasted_iota(jnp.int32, (PAIR, PAIR), 1)
    bdmask = (r2 >= HEAD_DIM) == (c2 >= HEAD_DIM)
    ones_bd = bdmask.astype(BF16)
    tr = lax.broadcasted_iota(jnp.int32, (L, L), 0)
    tc = lax.broadcasted_iota(jnp.int32, (L, L), 1)
    tri = (tc <= tr).astype(BF16)

    dot = lambda a, b: _dot(a.astype(BF16), b.astype(BF16))
    dot_nt = lambda a, b: _dot_nt(a.astype(BF16), b.astype(BF16))
    dot_tn = lambda a, b: _dot_tn(a.astype(BF16), b.astype(BF16))

    def bd(x):
        return jnp.where(bdmask, jnp.concatenate([x, x], axis=0), jnp.zeros((), x.dtype))

    def packdiag(full):
        return jnp.where(head0, full[0:L, :], full[L:2 * L, :])

    def each(f, *lists):
        return [f(*xs) for xs in zip(*lists)]

    def rows(c):
        return slice(c * L, (c + 1) * L)

    def lanes(p):
        return slice(p * PAIR, (p + 1) * PAIR)

    def prep(units, out):
        ld = lambda ref: [ref[0, rows(c), lanes(p)] for p, c in units]
        r, lw, k, v, kk, kb = ld(r_ref), ld(lw_ref), ld(k_ref), ld(v_ref), ld(kk_ref), ld(kb_ref)

        cs = each(lambda x: _dot(tri, jnp.concatenate(_hi_lo(x), axis=1)), lw)
        cs = each(lambda x: x[:, 0:PAIR] + x[:, PAIR:2 * PAIR], cs)
        yield
        cs_end = each(lambda x: x[L - 1:L, :], cs)
        kkd = each(lambda kk_, cs_, lw_: kk_ * jnp.exp(cs_ - lw_), kk, cs, lw)
        rd = each(lambda r_, cs_: r_ * jnp.exp(cs_), r, cs)
        p_inv = each(lambda cs_: jnp.exp(-cs_), cs)
        p_end = each(lambda ce, cs_: jnp.exp(ce - cs_), cs_end, cs)
        kh = each(lambda a, b: a * b, k, p_inv)
        bh = each(lambda a, b: a * b, kb, p_inv)
        ke = each(lambda a, b: a * b, k, p_end)
        be = each(lambda a, b: a * b, kb, p_end)

        def scores(kkd_, rd_, kh_, bh_):
            lhs = jnp.concatenate([kkd_, rd_], axis=0)
            rhs_t = jnp.concatenate([jnp.where(head0, kh_, 0.0), jnp.where(head0, 0.0, kh_),
                                     jnp.where(head0, bh_, 0.0), jnp.where(head0, 0.0, bh_)], axis=0)
            return dot_nt(lhs, rhs_t)

        aa = each(scores, kkd, rd, kh, bh)
        yield
        a1 = each(lambda x: jnp.where(strict, x[0:L, 0:PAIR], 0.0), aa)
        a2 = each(lambda x: jnp.where(strict, x[0:L, PAIR:2 * PAIR], 0.0), aa)
        a3 = each(lambda x: jnp.where(incl, x[L:2 * L, 0:PAIR], 0.0), aa)
        a4 = each(lambda x: jnp.where(incl, x[L:2 * L, PAIR:2 * PAIR], 0.0), aa)

        npow = each(lambda n: dot(n, bd(n)), a2)
        tm = each(lambda n: eye - n, a2)
        av = each(lambda x1, x3, v_: dot(jnp.concatenate([x1, x3], axis=0), bd(v_)), a1, a3, v)
        yield
        for _i in range(4):
            res = each(lambda t_, n: dot(jnp.concatenate([t_, n], axis=0), bd(n)), tm, npow)
            tm = each(lambda t_, x: t_ + x[0:L, :], tm, res)
            npow = each(lambda x: x[L:2 * L, :], res)
            yield
        tm = each(lambda t_, n: t_ + dot(t_, bd(n)), tm, npow)
        yield
        wu = each(lambda t_, kkd_, av_: dot(t_, jnp.concatenate([bd(kkd_), bd(av_[0:L, :])], axis=1)), tm, kkd, av)
        yield
        aw = each(lambda x4, wu_: dot(x4, jnp.concatenate([bd(wu_[:, 0:PAIR]), bd(wu_[:, PAIR:2 * PAIR])], axis=1)),
                  a4, wu)
        gm = each(lambda wu_, be_: dot_tn(wu_[:, 0:PAIR], be_), wu, be)
        yield
        cm = each(lambda v_, ke_, wu_, be_: dot_tn(jnp.concatenate([v_, wu_[:, PAIR:2 * PAIR]], axis=0),
                                                   jnp.concatenate([ke_, -be_], axis=0)), v, ke, wu, be)
        out["pm"] = each(lambda rd_, aw_: (rd_ - aw_[:, 0:PAIR]).astype(BF16), rd, aw)
        out["q"] = each(lambda av_, aw_: av_[L:2 * L, :] - aw_[:, PAIR:2 * PAIR], av, aw)
        out["g"] = each(lambda gm_: jnp.where(bdmask, -gm_, 0.0).astype(BF16), gm)
        out["cm"] = each(packdiag, cm)
        out["gam"] = each(jnp.exp, cs_end)
        yield

    def tail(units, fac, state):
        sb = []
        for j, (p, c) in enumerate(units):
            s = state[p]
            sb.append(s.astype(BF16))
            state[p] = s * fac["gam"][j] + _dot(sb[-1], fac["g"][j]) + fac["cm"][j]
            if p == npair - 1:
                yield
        o = [_dot_nt(pm_, bd(sb_)) + q_ for pm_, sb_, q_ in zip(fac["pm"], sb, fac["q"])]
        rk = [r_ref[0, rows(c), lanes(p)] * k_ref[0, rows(c), lanes(p)] * rk_ref[:, lanes(p)] for p, c in units]
        yield
        sums = each(lambda o_, rk_: _dot(jnp.concatenate([o_.astype(BF16), rk_.astype(BF16)], axis=0), ones_bd),
                    o, rk)
        yield
        dlt = each(lambda o_, s_: o_ - s_[0:L, :] * (1.0 / HEAD_DIM), o, sums)
        var = each(lambda d: _dot((d * d).astype(BF16), ones_bd) * (1.0 / HEAD_DIM), dlt)
        yield
        for (p, c), d, vr, s_ in zip(units, dlt, var, sums):
            on = d * lax.rsqrt(vr + GN_EPS) * lnw_ref[:, lanes(p)] + lnb_ref[:, lanes(p)]
            bonus = s_[L:2 * L, :] * v_ref[0, rows(c), lanes(p)]
            o_ref[0, rows(c), lanes(p)] = ((on + bonus) * g_ref[0, rows(c), lanes(p)]).astype(o_ref.dtype)
        yield

    def run(*gens):
        gens = list(gens)
        while gens:
            for gen in list(gens):
                try:
                    next(gen)
                except StopIteration:
                    gens.remove(gen)

    n_group = nc // group
    units = [[(p, gi * group + j) for j in range(group) for p in range(npair)] for gi in range(n_group)]
    state = [jnp.zeros((L, PAIR), F32) for _ in range(npair)]
    fac = {}
    run(prep(units[0], fac))
    for gi in range(n_group):
        nxt = {}
        if gi + 1 < n_group:
            run(tail(units[gi], fac, state), prep(units[gi + 1], nxt))
        else:
            run(tail(units[gi], fac, state))
        fac = nxt


def _wkv(r, lw, k, v, kk, kb, g, ln_w, ln_b, r_k, npair, group):
    b, t, dr = r.shape
    nc = t // CHUNK
    wl = npair * PAIR
    seq = pl.BlockSpec((1, t, wl), lambda i, j: (i, 0, j))
    par = pl.BlockSpec((1, wl), lambda i, j: (0, j))
    return pl.pallas_call(
        functools.partial(_wkv_kernel, nc, npair, group),
        grid=(b, dr // wl),
        in_specs=[seq] * 7 + [par] * 3,
        out_specs=seq,
        out_shape=jax.ShapeDtypeStruct((b, t, dr), BF16),
        compiler_params=pltpu.CompilerParams(dimension_semantics=("arbitrary", "arbitrary"),
                                             vmem_limit_bytes=VMEM_LIMIT),
        name="wkv",
    )(r, lw, k, v, kk, kb, g, ln_w, ln_b, r_k)


def _out_ffn_kernel(final_norm, cf, ahead, n_sub, x_ref, ya_ref, yb_ref, ma_ref, mb_ref, pa_ref, pb_ref, wo_ref,
                    g_ref, wu_ref, cw_ref, cb_ref, wd_ref, gf_ref, o_ref, zcar):
    @pl.when(pl.program_id(1) == 0)
    def _():
        zcar[...] = jnp.zeros(zcar.shape, F32)

    ts = x_ref.shape[1] // n_sub
    subs = [slice(i * ts, (i + 1) * ts) for i in range(n_sub)]
    pa = [_dot(ya_ref[0, s, :], pa_ref[...]) for s in subs]
    pb = [_dot(yb_ref[0, s, :], pb_ref[...]) for s in subs]
    acc = []
    for s, a, b in zip(subs, pa, pb):
        merged = ma_ref[0, s, :].astype(F32) * a + mb_ref[0, s, :].astype(F32) * b
        acc.append(x_ref[0, s, :] + _dot(merged.astype(BF16), wo_ref[...]))
    hb = [_rms(x, g_ref[...]).astype(BF16) for x in acc]
    f = wd_ref.shape[0]
    tasks = [(i, c) for i in range(n_sub) for c in range(f // cf)]

    def up(i, c):
        return [(_dot(hb[i], wu_ref[:, lo:lo + cf]), lo) for lo in (c * cf, f + c * cf)]

    def conv(z, lo):
        z1, z2 = _prev_rows(z, zcar, lo, lo + cf)
        cw = cw_ref[:, lo:lo + cf]
        return cw[0:1, :] * z2 + cw[1:2, :] * z1 + cw[2:3, :] * z + cb_ref[:, lo:lo + cf]

    pending = [up(*task) for task in tasks[:ahead]]
    for n, (i, c) in enumerate(tasks):
        if n + ahead < len(tasks):
            pending.append(up(*tasks[n + ahead]))
        zg, zu = [conv(z, lo) for z, lo in pending.pop(0)]
        act = zg * _sigmoid(zg) * zu
        acc[i] = acc[i] + _dot(act.astype(BF16), wd_ref[c * cf:(c + 1) * cf, :])
    for s, a in zip(subs, acc):
        o_ref[0, s, :] = _rms(a, gf_ref[...]) if final_norm else a


def _out_ffn(x, ya, yb, ma, mb, pa, pb, wo, norm_g, w_up, conv_w, conv_b, w_down, final_g, final_norm, tm, cf, ahead,
             n_sub):
    b, t, d = x.shape
    dr = ya.shape[-1]
    row = lambda w: pl.BlockSpec((1, tm, w), lambda i, j: (i, j, 0))
    consts = [pa, pb, wo, norm_g, w_up, conv_w, conv_b, w_down, final_g]
    return pl.pallas_call(
        functools.partial(_out_ffn_kernel, final_norm, cf, ahead, n_sub),
        grid=(b, t // tm),
        in_specs=[row(d), row(dr), row(dr), row(d), row(d)] + [_const_spec(c.shape) for c in consts],
        out_specs=row(d),
        out_shape=jax.ShapeDtypeStruct((b, t, d), F32),
        scratch_shapes=[pltpu.VMEM((CARRY, w_up.shape[1]), F32)],
        compiler_params=pltpu.CompilerParams(dimension_semantics=("arbitrary", "arbitrary"),
                                             vmem_limit_bytes=VMEM_LIMIT),
        name="out_ffn_final" if final_norm else "out_ffn",
    )(x, ya, yb, ma, mb, *consts)


def _pad_rows(w, rows, offset=0):
    out = jnp.zeros((rows, w.shape[1]), w.dtype)
    return out.at[offset:offset + w.shape[0]].set(w)


def _forward(x, norm_mix_g, w_in, mu_shift, w0, decay_up, a0, a_up, g_up, k_k, k_a, r_k, ln_x_w, ln_x_b,
             vres_down, vres_up, v0, conv_w, proj_a, proj_b, w_out, norm_ffn_g, w_up, ffn_conv_w, ffn_conv_b,
             w_down, norm_final_g, *, mix_cfg, ffn_cfg, wkv_cfg):
    b, t, d = x.shape
    depth = w_in.shape[0]
    dr = w0.shape[-1]
    shift_cols = mu_shift.shape[-1]
    head = jnp.arange(dr) // HEAD_DIM
    ones_h = (head[:, None] == head[None, :]).astype(BF16)
    row = lambda p: p.reshape(1, -1)
    v_first = None
    for l in range(depth):
        win = w_in[l].astype(BF16)
        du = _pad_rows(decay_up[l], LANE).astype(BF16)
        au = _pad_rows(a_up[l], LANE, DECAY_LORA).astype(BF16)
        vres = None
        if l > 0:
            vd = jnp.pad(vres_down[l - 1], ((0, 0), (0, LANE - VRES_LORA))).astype(BF16)
            vu = _pad_rows(vres_up[l - 1], LANE).astype(BF16)
            vres = (vd, vu, row(v0[l - 1]), v_first)
        r, lw, k, v, kk, kb, g, yb, ma, mb = _mix_in(
            x, row(norm_mix_g[l]), win[:, :shift_cols], win[:, shift_cols:], row(mu_shift[l]), row(w0[l]), du,
            row(a0[l]), au, g_up[l].astype(BF16), row(k_k[l]), row(k_a[l]), conv_w[l], ones_h, vres, *mix_cfg)
        if l == 0:
            v_first = v
        ya = _wkv(r, lw, k, v, kk, kb, g, row(ln_x_w[l]), row(ln_x_b[l]), row(r_k[l]), *wkv_cfg)
        x = _out_ffn(x, ya, yb, ma, mb, proj_a[l].astype(BF16), proj_b[l].astype(BF16), w_out[l].astype(BF16),
                     row(norm_ffn_g[l]), w_up[l].astype(BF16), ffn_conv_w[l], row(ffn_conv_b[l]),
                     w_down[l].astype(BF16), row(norm_final_g), l == depth - 1, *ffn_cfg)
    return x


def kernel(x, norm_mix_g, w_in, mu_shift, w0, decay_up, a0, a_up, g_up, k_k, k_a, r_k, ln_x_w, ln_x_b, vres_down, vres_up, v0, conv_w, proj_a, proj_b, w_out, norm_ffn_g, w_up, ffn_conv_w, ffn_conv_b, w_down, norm_final_g):
    return _forward(x, norm_mix_g, w_in, mu_shift, w0, decay_up, a0, a_up, g_up, k_k, k_a, r_k, ln_x_w, ln_x_b,
                    vres_down, vres_up, v0, conv_w, proj_a, proj_b, w_out, norm_ffn_g, w_up, ffn_conv_w,
                    ffn_conv_b, w_down, norm_final_g,
                    mix_cfg=(512, 2), ffn_cfg=(512, 256, 3, 2), wkv_cfg=(2, 8))
```

```python
import functools

import jax
import jax.numpy as jnp
from jax import lax
from jax.experimental import pallas as pl
from jax.experimental.pallas import tpu as pltpu

F32 = jnp.float32
BF16 = jnp.bfloat16

HEAD_DIM = 64
CHUNK = 64
PAIR = 2 * HEAD_DIM
DECAY_LORA = 64
ICLR_LORA = 64
GATE_LORA = 128
VRES_LORA = 32
LANE = 128
CARRY = 8
RMS_EPS = 1e-6
GN_EPS = HEAD_DIM * 1e-5
VMEM_LIMIT = 56 * 1024 * 1024


def _dot(a, b):
    return jnp.dot(a, b, preferred_element_type=F32)


def _dot_nt(a, b):
    return lax.dot_general(a, b, (((1,), (1,)), ((), ())), preferred_element_type=F32)


def _dot_tn(a, b):
    return lax.dot_general(a, b, (((0,), (0,)), ((), ())), preferred_element_type=F32)


def _hi_lo(x):
    hi = x.astype(BF16)
    return hi, (x - hi.astype(F32)).astype(BF16)


def _rms(x, g):
    return x * lax.rsqrt(jnp.mean(x * x, axis=-1, keepdims=True) + RMS_EPS) * g


def _sigmoid(x):
    return 1.0 / (1.0 + jnp.exp(-x))


def _prev_rows(z, carry_ref, lo, hi):
    tm = z.shape[0]
    prev = carry_ref[:, lo:hi]
    carry_ref[:, lo:hi] = z[tm - CARRY:tm, :]
    row = lax.broadcasted_iota(jnp.int32, prev.shape, 0)
    out = []
    for shift in (1, 2):
        rolled = pltpu.roll(z, shift, 0)
        first = jnp.where(row < shift, pltpu.roll(prev, shift, 0), rolled[0:CARRY, :])
        out.append(jnp.concatenate([first, rolled[CARRY:, :]], axis=0))
    return out


def _mix_in_kernel(has_vres, d_rwkv, n_sub, *refs):
    if has_vres:
        (x_ref, g_ref, win_ref, mu_ref, w0_ref, du_ref, a0_ref, au_ref, gu_ref, kk_ref, ka_ref,
         cw_ref, ones_ref, vd_ref, vu_ref, v0_ref, vf_ref,
         r_out, lw_out, k_out, v_out, kkn_out, kb_out, g_out, yb_out, ma_out, mb_out, scar, ccar) = refs
    else:
        (x_ref, g_ref, win_ref, mu_ref, w0_ref, du_ref, a0_ref, au_ref, gu_ref, kk_ref, ka_ref,
         cw_ref, ones_ref,
         r_out, lw_out, k_out, v_out, kkn_out, kb_out, g_out, yb_out, ma_out, mb_out, scar, ccar) = refs

    @pl.when(pl.program_id(1) == 0)
    def _():
        scar[...] = jnp.zeros(scar.shape, F32)
        ccar[...] = jnp.zeros(ccar.shape, F32)

    dr = d_rwkv
    d = ma_out.shape[-1]
    sc = mu_ref.shape[-1]
    ts = x_ref.shape[1] // n_sub
    subs = [slice(i * ts, (i + 1) * ts) for i in range(n_sub)]
    hbs = [_rms(x_ref[0, s, :], g_ref[...]).astype(BF16) for s in subs]

    for s, hb in zip(subs, hbs):
        ps = _dot(hb, win_ref[:, 0:sc])
        ma_out[0, s, :] = _sigmoid(_dot(hb, win_ref[:, sc + 3 * dr:sc + 3 * dr + d])).astype(ma_out.dtype)
        mb_out[0, s, :] = _sigmoid(_dot(hb, win_ref[:, sc + 3 * dr + d:sc + 3 * dr + 2 * d])).astype(mb_out.dtype)

        prev, _ = _prev_rows(ps, scar, 0, ps.shape[1])
        sh = ps + (prev - ps) * mu_ref[...]
        r = sh[:, 0:dr]
        k = sh[:, dr:2 * dr]
        v = sh[:, 2 * dr:3 * dr]
        wa = sh[:, 3 * dr:3 * dr + LANE]
        gd = sh[:, 3 * dr + LANE:3 * dr + 2 * LANE]

        y = -(w0_ref[...] + _dot(jnp.tanh(wa).astype(BF16), du_ref[...]))
        softplus = jnp.maximum(y, 0.0) + jnp.log(1.0 + jnp.exp(-jnp.abs(y)))
        lw_out[0, s, :] = -jnp.exp(-softplus - 0.5)
        a = _sigmoid(a0_ref[...] + _dot(wa.astype(BF16), au_ref[...]))
        g_out[0, s, :] = _dot(_sigmoid(gd).astype(BF16), gu_ref[...]).astype(g_out.dtype)
        if has_vres:
            lo = _dot(hb, vd_ref[...])
            mix = _sigmoid(v0_ref[...] + _dot(lo.astype(BF16), vu_ref[...]))
            v = v + (vf_ref[0, s, :].astype(F32) - v) * mix
        kk = k * kk_ref[...]
        sq = (kk * kk).astype(BF16)
        hw = ones_ref.shape[0]
        nrm = jnp.sqrt(jnp.concatenate([_dot(sq[:, lo:lo + hw], ones_ref[...]) for lo in range(0, dr, hw)], axis=1))
        kk = kk / jnp.maximum(nrm, 1e-12)
        r_out[0, s, :] = r.astype(r_out.dtype)
        k_out[0, s, :] = (k * (1.0 + (a - 1.0) * ka_ref[...])).astype(k_out.dtype)
        v_out[0, s, :] = v.astype(v_out.dtype)
        kkn_out[0, s, :] = kk.astype(kkn_out.dtype)
        kb_out[0, s, :] = (kk * a).astype(kb_out.dtype)

        pc = _dot(hb, win_ref[:, sc:sc + 3 * dr])
        cu = pc[:, 2 * dr:3 * dr] * pc[:, 0:dr]
        c1, c2 = _prev_rows(cu, ccar, 0, dr)
        cw = cw_ref[...]
        yb_out[0, s, :] = (pc[:, dr:2 * dr] * (cw[0:1, :] * c2 + cw[1:2, :] * c1 + cw[2:3, :] * cu)
                           ).astype(yb_out.dtype)


def _const_spec(shape):
    nd = len(shape)
    return pl.BlockSpec(shape, lambda *_: (0,) * nd, pipeline_mode=pl.Buffered(1))


def _mix_in(x, norm_g, win, mu, w0, du, a0, au, gu, k_k, k_a, conv_w, ones_h, vres, tm, n_sub):
    b, t, d = x.shape
    dr = w0.shape[-1]
    has_vres = vres is not None
    row = lambda w: pl.BlockSpec((1, tm, w), lambda i, j: (i, j, 0))
    consts = [norm_g, win, mu, w0, du, a0, au, gu, k_k, k_a, conv_w, ones_h]
    in_specs = [row(d)] + [_const_spec(c.shape) for c in consts]
    args = [x] + consts
    if has_vres:
        vd, vu, v0, v_first = vres
        in_specs += [_const_spec(vd.shape), _const_spec(vu.shape), _const_spec(v0.shape), row(dr)]
        args += [vd, vu, v0, v_first]
    wide = lambda w, dt: jax.ShapeDtypeStruct((b, t, w), dt)
    out_shape = ([wide(dr, BF16), wide(dr, F32)] + [wide(dr, BF16)] * 6 + [wide(d, BF16)] * 2)
    out_specs = [row(dr)] * 8 + [row(d)] * 2
    return pl.pallas_call(
        functools.partial(_mix_in_kernel, has_vres, dr, n_sub),
        grid=(b, t // tm),
        in_specs=in_specs,
        out_specs=out_specs,
        out_shape=out_shape,
        scratch_shapes=[pltpu.VMEM((CARRY, mu.shape[1]), F32), pltpu.VMEM((CARRY, dr), F32)],
        compiler_params=pltpu.CompilerParams(dimension_semantics=("arbitrary", "arbitrary"),
                                             vmem_limit_bytes=VMEM_LIMIT),
        name="mix_in_vres" if has_vres else "mix_in",
    )(*args)


def _wkv_kernel(nc, npair, groups, r_ref, lw_ref, k_ref, v_ref, kk_ref, kb_ref, g_ref, lnw_ref, lnb_ref, rk_ref,
                o_ref):
    L = CHUNK
    row = lax.broadcasted_iota(jnp.int32, (L, PAIR), 0)
    col = lax.broadcasted_iota(jnp.int32, (L, PAIR), 1)
    pos = col & (HEAD_DIM - 1)
    strict = pos < row
    incl = pos <= row
    eye = (pos == row).astype(F32)
    head0 = col < HEAD_DIM
    r2 = lax.broadcasted_iota(jnp.int32, (PAIR, PAIR), 0)
    c2 = lax.broadcasted_iota(jnp.int32, (PAIR, PAIR), 1)
    bdmask = (r2 >= HEAD_DIM) == (c2 >= HEAD_DIM)
    ones_bd = bdmask.astype(BF16)
    tr = lax.broadcasted_iota(jnp.int32, (L, L), 0)
    tc = lax.broadcasted_iota(jnp.int32, (L, L), 1)
    tri = (tc <= tr).astype(BF16)

    dot = lambda a, b: _dot(a.astype(BF16), b.astype(BF16))
    dot_nt = lambda a, b: _dot_nt(a.astype(BF16), b.astype(BF16))
    dot_tn = lambda a, b: _dot_tn(a.astype(BF16), b.astype(BF16))

    def bd(x):
        return jnp.where(bdmask, jnp.concatenate([x, x], axis=0), jnp.zeros((), x.dtype))

    def packdiag(full):
        return jnp.where(head0, full[0:L, :], full[L:2 * L, :])

    def each(f, *lists):
        return [f(*xs) for xs in zip(*lists)]

    def rows(c):
        return slice(c * L, (c + 1) * L)

    def lanes(p):
        return slice(p * PAIR, (p + 1) * PAIR)

    def prep(units, out):
        ld = lambda ref: [ref[0, rows(c), lanes(p)].astype(F32) for p, c in units]
        r, lw, k, v, kk, kb = ld(r_ref), ld(lw_ref), ld(k_ref), ld(v_ref), ld(kk_ref), ld(kb_ref)

        cs = each(lambda x: _dot(tri, jnp.concatenate(_hi_lo(x), axis=1)), lw)
        cs = each(lambda x: x[:, 0:PAIR] + x[:, PAIR:2 * PAIR], cs)
        yield
        cs_end = each(lambda x: x[L - 1:L, :], cs)
        kkd = each(lambda kk_, cs_, lw_: kk_ * jnp.exp(cs_ - lw_), kk, cs, lw)
        rd = each(lambda r_, cs_: r_ * jnp.exp(cs_), r, cs)
        p_inv = each(lambda cs_: jnp.exp(-cs_), cs)
        p_end = each(lambda ce, cs_: jnp.exp(ce - cs_), cs_end, cs)
        kh = each(lambda a, b: a * b, k, p_inv)
        bh = each(lambda a, b: a * b, kb, p_inv)
        ke = each(lambda a, b: a * b, k, p_end)
        be = each(lambda a, b: a * b, kb, p_end)

        def scores(kkd_, rd_, kh_, bh_):
            lhs = jnp.concatenate([kkd_, rd_], axis=0)
            rhs_t = jnp.concatenate([jnp.where(head0, kh_, 0.0), jnp.where(head0, 0.0, kh_),
                                     jnp.where(head0, bh_, 0.0), jnp.where(head0, 0.0, bh_)], axis=0)
            return dot_nt(lhs, rhs_t)

        aa = each(scores, kkd, rd, kh, bh)
        yield
        a1 = each(lambda x: jnp.where(strict, x[0:L, 0:PAIR], 0.0), aa)
        a2 = each(lambda x: jnp.where(strict, x[0:L, PAIR:2 * PAIR], 0.0), aa)
        a3 = each(lambda x: jnp.where(incl, x[L:2 * L, 0:PAIR], 0.0), aa)
        a4 = each(lambda x: jnp.where(incl, x[L:2 * L, PAIR:2 * PAIR], 0.0), aa)

        npow = each(lambda n: dot(n, bd(n)), a2)
        tm = each(lambda n: eye - n, a2)
        av = each(lambda x1, x3, v_: dot(jnp.concatenate([x1, x3], axis=0), bd(v_)), a1, a3, v)
        yield
        for _i in range(4):
            res = each(lambda t_, n: dot(jnp.concatenate([t_, n], axis=0), bd(n)), tm, npow)
            tm = each(lambda t_, x: t_ + x[0:L, :], tm, res)
            npow = each(lambda x: x[L:2 * L, :], res)
            yield
        tm = each(lambda t_, n: t_ + dot(t_, bd(n)), tm, npow)
        yield
        wu = each(lambda t_, kkd_, av_: dot(t_, jnp.concatenate([bd(kkd_), bd(av_[0:L, :])], axis=1)), tm, kkd, av)
        yield
        aw = each(lambda x4, wu_: dot(x4, jnp.concatenate([bd(wu_[:, 0:PAIR]), bd(wu_[:, PAIR:2 * PAIR])], axis=1)),
                  a4, wu)
        gm = each(lambda wu_, be_: dot_tn(wu_[:, 0:PAIR], be_), wu, be)
        yield
        cm = each(lambda v_, ke_, wu_, be_: dot_tn(jnp.concatenate([v_, wu_[:, PAIR:2 * PAIR]], axis=0),
                                                   jnp.concatenate([ke_, -be_], axis=0)), v, ke, wu, be)
        out["pm"] = each(lambda rd_, aw_: (rd_ - aw_[:, 0:PAIR]).astype(BF16), rd, aw)
        out["q"] = each(lambda av_, aw_: av_[L:2 * L, :] - aw_[:, PAIR:2 * PAIR], av, aw)
        out["g"] = each(lambda gm_: jnp.where(bdmask, -gm_, 0.0).astype(BF16), gm)
        out["cm"] = each(packdiag, cm)
        out["gam"] = each(jnp.exp, cs_end)
        yield

    def tail(units, fac, state):
        sb = []
        for j, (p, c) in enumerate(units):
            s = state[p]
            sb.append(s.astype(BF16))
            state[p] = s * fac["gam"][j] + _dot(sb[-1], fac["g"][j]) + fac["cm"][j]
            if p == npair - 1:
                yield
        o = [_dot_nt(pm_, bd(sb_)) + q_ for pm_, sb_, q_ in zip(fac["pm"], sb, fac["q"])]
        rk = [r_ref[0, rows(c), lanes(p)].astype(F32) * k_ref[0, rows(c), lanes(p)].astype(F32) * rk_ref[:, lanes(p)]
              for p, c in units]
        yield
        sums = each(lambda o_, rk_: _dot(jnp.concatenate([o_.astype(BF16), rk_.astype(BF16)], axis=0), ones_bd),
                    o, rk)
        yield
        dlt = each(lambda o_, s_: o_ - s_[0:L, :] * (1.0 / HEAD_DIM), o, sums)
        var = each(lambda d: _dot((d * d).astype(BF16), ones_bd) * (1.0 / HEAD_DIM), dlt)
        yield
        for (p, c), d, vr, s_ in zip(units, dlt, var, sums):
            on = d * lax.rsqrt(vr + GN_EPS) * lnw_ref[:, lanes(p)] + lnb_ref[:, lanes(p)]
            bonus = s_[L:2 * L, :] * v_ref[0, rows(c), lanes(p)].astype(F32)
            o_ref[0, rows(c), lanes(p)] = ((on + bonus) * g_ref[0, rows(c), lanes(p)]).astype(o_ref.dtype)
        yield

    def run(*gens):
        gens = list(gens)
        while gens:
            for gen in list(gens):
                try:
                    next(gen)
                except StopIteration:
                    gens.remove(gen)

    assert sum(groups) == nc, (groups, nc)
    starts = [sum(groups[:gi]) for gi in range(len(groups))]
    n_group = len(groups)
    units = [[(p, st + j) for j in range(sz) for p in range(npair)] for st, sz in zip(starts, groups)]
    state = [jnp.zeros((L, PAIR), F32) for _ in range(npair)]
    fac = {}
    run(prep(units[0], fac))
    for gi in range(n_group):
        nxt = {}
        if gi + 1 < n_group:
            run(tail(units[gi], fac, state), prep(units[gi + 1], nxt))
        else:
            run(tail(units[gi], fac, state))
        fac = nxt


def _wkv(r, lw, k, v, kk, kb, g, ln_w, ln_b, r_k, npair, groups):
    b, t, dr = r.shape
    nc = t // CHUNK
    wl = npair * PAIR
    seq = pl.BlockSpec((1, t, wl), lambda i, j: (i, 0, j))
    par = pl.BlockSpec((1, wl), lambda i, j: (0, j))
    return pl.pallas_call(
        functools.partial(_wkv_kernel, nc, npair, groups),
        grid=(b, dr // wl),
        in_specs=[seq] * 7 + [par] * 3,
        out_specs=seq,
        out_shape=jax.ShapeDtypeStruct((b, t, dr), BF16),
        compiler_params=pltpu.CompilerParams(dimension_semantics=("arbitrary", "arbitrary"),
                                             vmem_limit_bytes=VMEM_LIMIT),
        name="wkv",
    )(r, lw, k, v, kk, kb, g, ln_w, ln_b, r_k)


def _out_ffn_kernel(final_norm, cf, ahead, n_sub, x_ref, ya_ref, yb_ref, ma_ref, mb_ref, pa_ref, pb_ref, wo_ref,
                    g_ref, wu_ref, cw_ref, cb_ref, wd_ref, gf_ref, o_ref, zcar):
    @pl.when(pl.program_id(1) == 0)
    def _():
        zcar[...] = jnp.zeros(zcar.shape, F32)

    ts = x_ref.shape[1] // n_sub
    subs = [slice(i * ts, (i + 1) * ts) for i in range(n_sub)]
    pa = [_dot(ya_ref[0, s, :], pa_ref[...]) for s in subs]
    pb = [_dot(yb_ref[0, s, :], pb_ref[...]) for s in subs]
    acc = []
    for s, a, b in zip(subs, pa, pb):
        merged = ma_ref[0, s, :].astype(F32) * a + mb_ref[0, s, :].astype(F32) * b
        acc.append(x_ref[0, s, :] + _dot(merged.astype(BF16), wo_ref[...]))
    hb = [_rms(x, g_ref[...]).astype(BF16) for x in acc]
    f = wd_ref.shape[0]
    tasks = [(i, c) for i in range(n_sub) for c in range(f // cf)]

    def up(i, c):
        return [(_dot(hb[i], wu_ref[:, lo:lo + cf]), lo) for lo in (c * cf, f + c * cf)]

    def conv(z, lo):
        z1, z2 = _prev_rows(z, zcar, lo, lo + cf)
        cw = cw_ref[:, lo:lo + cf]
        return cw[0:1, :] * z2 + cw[1:2, :] * z1 + cw[2:3, :] * z + cb_ref[:, lo:lo + cf]

    pending = [up(*task) for task in tasks[:ahead]]
    for n, (i, c) in enumerate(tasks):
        if n + ahead < len(tasks):
            pending.append(up(*tasks[n + ahead]))
        zg, zu = [conv(z, lo) for z, lo in pending.pop(0)]
        act = zg * _sigmoid(zg) * zu
        acc[i] = acc[i] + _dot(act.astype(BF16), wd_ref[c * cf:(c + 1) * cf, :])
    for s, a in zip(subs, acc):
        o_ref[0, s, :] = _rms(a, gf_ref[...]) if final_norm else a


def _out_ffn(x, ya, yb, ma, mb, pa, pb, wo, norm_g, w_up, conv_w, conv_b, w_down, final_g, final_norm, tm, cf, ahead,
             n_sub):
    b, t, d = x.shape
    dr = ya.shape[-1]
    row = lambda w: pl.BlockSpec((1, tm, w), lambda i, j: (i, j, 0))
    consts = [pa, pb, wo, norm_g, w_up, conv_w, conv_b, w_down, final_g]
    return pl.pallas_call(
        functools.partial(_out_ffn_kernel, final_norm, cf, ahead, n_sub),
        grid=(b, t // tm),
        in_specs=[row(d), row(dr), row(dr), row(d), row(d)] + [_const_spec(c.shape) for c in consts],
        out_specs=row(d),
        out_shape=jax.ShapeDtypeStruct((b, t, d), F32),
        scratch_shapes=[pltpu.VMEM((CARRY, w_up.shape[1]), F32)],
        compiler_params=pltpu.CompilerParams(dimension_semantics=("arbitrary", "arbitrary"),
                                             vmem_limit_bytes=VMEM_LIMIT),
        name="out_ffn_final" if final_norm else "out_ffn",
    )(x, ya, yb, ma, mb, *consts)


def _pad_rows(w, rows, offset=0):
    out = jnp.zeros((rows, w.shape[1]), w.dtype)
    return out.at[offset:offset + w.shape[0]].set(w)


def _forward(x, norm_mix_g, w_in, mu_shift, w0, decay_up, a0, a_up, g_up, k_k, k_a, r_k, ln_x_w, ln_x_b,
             vres_down, vres_up, v0, conv_w, proj_a, proj_b, w_out, norm_ffn_g, w_up, ffn_conv_w, ffn_conv_b,
             w_down, norm_final_g, *, mix_cfg, ffn_cfg, wkv_cfg):
    b, t, d = x.shape
    depth = w_in.shape[0]
    dr = w0.shape[-1]
    head = jnp.arange(2 * PAIR) // HEAD_DIM
    ones_h = (head[:, None] == head[None, :]).astype(BF16)
    row = lambda p: p.reshape(1, -1)
    v_first = None
    for l in range(depth):
        win = w_in[l].astype(BF16)
        du = _pad_rows(decay_up[l], LANE).astype(BF16)
        au = _pad_rows(a_up[l], LANE, DECAY_LORA).astype(BF16)
        vres = None
        if l > 0:
            vd = jnp.pad(vres_down[l - 1], ((0, 0), (0, LANE - VRES_LORA))).astype(BF16)
            vu = _pad_rows(vres_up[l - 1], LANE).astype(BF16)
            vres = (vd, vu, row(v0[l - 1]), v_first)
        r, lw, k, v, kk, kb, g, yb, ma, mb = _mix_in(
            x, row(norm_mix_g[l]), win, row(mu_shift[l]), row(w0[l]), du,
            row(a0[l]), au, g_up[l].astype(BF16), row(k_k[l]), row(k_a[l]), conv_w[l], ones_h, vres, *mix_cfg)
        if l == 0:
            v_first = v
        ya = _wkv(r, lw, k, v, kk, kb, g, row(ln_x_w[l]), row(ln_x_b[l]), row(r_k[l]), *wkv_cfg)
        x = _out_ffn(x, ya, yb, ma, mb, proj_a[l].astype(BF16), proj_b[l].astype(BF16), w_out[l].astype(BF16),
                     row(norm_ffn_g[l]), w_up[l].astype(BF16), ffn_conv_w[l], row(ffn_conv_b[l]),
                     w_down[l].astype(BF16), row(norm_final_g), l == depth - 1, *ffn_cfg)
    return x


def kernel(x, norm_mix_g, w_in, mu_shift, w0, decay_up, a0, a_up, g_up, k_k, k_a, r_k, ln_x_w, ln_x_b, vres_down, vres_up, v0, conv_w, proj_a, proj_b, w_out, norm_ffn_g, w_up, ffn_conv_w, ffn_conv_b, w_down, norm_final_g):
    return _forward(x, norm_mix_g, w_in, mu_shift, w0, decay_up, a0, a_up, g_up, k_k, k_a, r_k, ln_x_w, ln_x_b,
                    vres_down, vres_up, v0, conv_w, proj_a, proj_b, w_out, norm_ffn_g, w_up, ffn_conv_w,
                    ffn_conv_b, w_down, norm_final_g,
                    mix_cfg=(512, 2), ffn_cfg=(512, 256, 3, 2), wkv_cfg=(2, (8, 8, 8, 8)))
```

```python
import functools

import jax
import jax.numpy as jnp
from jax import lax
from jax.experimental import pallas as pl
from jax.experimental.pallas import tpu as pltpu

F32 = jnp.float32
BF16 = jnp.bfloat16

HEAD_DIM = 64
CHUNK = 64
PAIR = 2 * HEAD_DIM
DECAY_LORA = 64
ICLR_LORA = 64
GATE_LORA = 128
VRES_LORA = 32
LANE = 128
CARRY = 8
RMS_EPS = 1e-6
GN_EPS = HEAD_DIM * 1e-5
VMEM_LIMIT = 56 * 1024 * 1024


def _dot(a, b):
    return jnp.dot(a, b, preferred_element_type=F32)


def _dot_nt(a, b):
    return lax.dot_general(a, b, (((1,), (1,)), ((), ())), preferred_element_type=F32)


def _dot_tn(a, b):
    return lax.dot_general(a, b, (((0,), (0,)), ((), ())), preferred_element_type=F32)


def _hi_lo(x):
    hi = x.astype(BF16)
    return hi, (x - hi.astype(F32)).astype(BF16)


def _rms(x, g):
    return x * lax.rsqrt(jnp.mean(x * x, axis=-1, keepdims=True) + RMS_EPS) * g


def _sigmoid(x):
    return 1.0 / (1.0 + jnp.exp(-x))


def _prev_rows(z, carry_ref, lo, hi):
    tm = z.shape[0]
    prev = carry_ref[:, lo:hi]
    carry_ref[:, lo:hi] = z[tm - CARRY:tm, :]
    row = lax.broadcasted_iota(jnp.int32, prev.shape, 0)
    out = []
    for shift in (1, 2):
        rolled = pltpu.roll(z, shift, 0)
        first = jnp.where(row < shift, pltpu.roll(prev, shift, 0), rolled[0:CARRY, :])
        out.append(jnp.concatenate([first, rolled[CARRY:, :]], axis=0))
    return out


def _mix_in_kernel(has_vres, d_rwkv, n_sub, *refs):
    if has_vres:
        (x_ref, g_ref, win_ref, mu_ref, w0_ref, du_ref, a0_ref, au_ref, gu_ref, kk_ref, ka_ref,
         cw_ref, ones_ref, vd_ref, vu_ref, v0_ref, vf_ref,
         r_out, lw_out, k_out, v_out, kkn_out, kb_out, g_out, yb_out, ma_out, mb_out, scar, ccar) = refs
    else:
        (x_ref, g_ref, win_ref, mu_ref, w0_ref, du_ref, a0_ref, au_ref, gu_ref, kk_ref, ka_ref,
         cw_ref, ones_ref,
         r_out, lw_out, k_out, v_out, kkn_out, kb_out, g_out, yb_out, ma_out, mb_out, scar, ccar) = refs

    @pl.when(pl.program_id(1) == 0)
    def _():
        scar[...] = jnp.zeros(scar.shape, F32)
        ccar[...] = jnp.zeros(ccar.shape, F32)

    dr = d_rwkv
    d = ma_out.shape[-1]
    sc = mu_ref.shape[-1]
    ts = x_ref.shape[1] // n_sub
    subs = [slice(i * ts, (i + 1) * ts) for i in range(n_sub)]
    hbs = [_rms(x_ref[0, s, :], g_ref[...]).astype(BF16) for s in subs]

    for s, hb in zip(subs, hbs):
        ps = _dot(hb, win_ref[:, 0:sc])
        ma_out[0, s, :] = _sigmoid(_dot(hb, win_ref[:, sc + 3 * dr:sc + 3 * dr + d])).astype(ma_out.dtype)
        mb_out[0, s, :] = _sigmoid(_dot(hb, win_ref[:, sc + 3 * dr + d:sc + 3 * dr + 2 * d])).astype(mb_out.dtype)

        prev, _ = _prev_rows(ps, scar, 0, ps.shape[1])
        sh = ps + (prev - ps) * mu_ref[...]
        r = sh[:, 0:dr]
        k = sh[:, dr:2 * dr]
        v = sh[:, 2 * dr:3 * dr]
        wa = sh[:, 3 * dr:3 * dr + LANE]
        gd = sh[:, 3 * dr + LANE:3 * dr + 2 * LANE]

        y = -(w0_ref[...] + _dot(jnp.tanh(wa).astype(BF16), du_ref[...]))
        softplus = jnp.maximum(y, 0.0) + jnp.log(1.0 + jnp.exp(-jnp.abs(y)))
        lw_out[0, s, :] = -jnp.exp(-softplus - 0.5)
        a = _sigmoid(a0_ref[...] + _dot(wa.astype(BF16), au_ref[...]))
        g_out[0, s, :] = _dot(_sigmoid(gd).astype(BF16), gu_ref[...]).astype(g_out.dtype)
        if has_vres:
            lo = _dot(hb, vd_ref[...])
            mix = _sigmoid(v0_ref[...] + _dot(lo.astype(BF16), vu_ref[...]))
            v = v + (vf_ref[0, s, :].astype(F32) - v) * mix
        kk = k * kk_ref[...]
        sq = (kk * kk).astype(BF16)
        hw = ones_ref.shape[0]
        nrm = jnp.sqrt(jnp.concatenate([_dot(sq[:, lo:lo + hw], ones_ref[...]) for lo in range(0, dr, hw)], axis=1))
        kk = kk / jnp.maximum(nrm, 1e-12)
        r_out[0, s, :] = r.astype(r_out.dtype)
        k_out[0, s, :] = (k * (1.0 + (a - 1.0) * ka_ref[...])).astype(k_out.dtype)
        v_out[0, s, :] = v.astype(v_out.dtype)
        kkn_out[0, s, :] = kk.astype(kkn_out.dtype)
        kb_out[0, s, :] = (kk * a).astype(kb_out.dtype)

        pc = _dot(hb, win_ref[:, sc:sc + 3 * dr])
        cu = pc[:, 2 * dr:3 * dr] * pc[:, 0:dr]
        c1, c2 = _prev_rows(cu, ccar, 0, dr)
        cw = cw_ref[...]
        yb_out[0, s, :] = (pc[:, dr:2 * dr] * (cw[0:1, :] * c2 + cw[1:2, :] * c1 + cw[2:3, :] * cu)
                           ).astype(yb_out.dtype)


def _const_spec(shape):
    nd = len(shape)
    return pl.BlockSpec(shape, lambda *_: (0,) * nd, pipeline_mode=pl.Buffered(1))


def _mix_in(x, norm_g, win, mu, w0, du, a0, au, gu, k_k, k_a, conv_w, ones_h, vres, tm, n_sub):
    b, t, d = x.shape
    dr = w0.shape[-1]
    has_vres = vres is not None
    row = lambda w: pl.BlockSpec((1, tm, w), lambda i, j: (i, j, 0))
    consts = [norm_g, win, mu, w0, du, a0, au, gu, k_k, k_a, conv_w, ones_h]
    in_specs = [row(d)] + [_const_spec(c.shape) for c in consts]
    args = [x] + consts
    if has_vres:
        vd, vu, v0, v_first = vres
        in_specs += [_const_spec(vd.shape), _const_spec(vu.shape), _const_spec(v0.shape), row(dr)]
        args += [vd, vu, v0, v_first]
    wide = lambda w, dt: jax.ShapeDtypeStruct((b, t, w), dt)
    out_shape = [wide(dr, F32)] * 6 + [wide(dr, BF16)] * 2 + [wide(d, BF16)] * 2
    out_specs = [row(dr)] * 8 + [row(d)] * 2
    return pl.pallas_call(
        functools.partial(_mix_in_kernel, has_vres, dr, n_sub),
        grid=(b, t // tm),
        in_specs=in_specs,
        out_specs=out_specs,
        out_shape=out_shape,
        scratch_shapes=[pltpu.VMEM((CARRY, mu.shape[1]), F32), pltpu.VMEM((CARRY, dr), F32)],
        compiler_params=pltpu.CompilerParams(dimension_semantics=("arbitrary", "arbitrary"),
                                             vmem_limit_bytes=VMEM_LIMIT),
        name="mix_in_vres" if has_vres else "mix_in",
    )(*args)


def _wkv_kernel(nc, npair, set_pairs, groups, r_ref, lw_ref, k_ref, v_ref, kk_ref, kb_ref, g_ref, lnw_ref, lnb_ref, rk_ref,
                o_ref):
    L = CHUNK
    row = lax.broadcasted_iota(jnp.int32, (L, PAIR), 0)
    col = lax.broadcasted_iota(jnp.int32, (L, PAIR), 1)
    pos = col & (HEAD_DIM - 1)
    strict = pos < row
    incl = pos <= row
    eye = (pos == row).astype(F32)
    halves = [jnp.logical_and((row >> i) & 1 == 1, (pos >> i) == (row >> i) - 1) for i in range(L.bit_length() - 1)]
    head0 = col < HEAD_DIM
    r2 = lax.broadcasted_iota(jnp.int32, (PAIR, PAIR), 0)
    c2 = lax.broadcasted_iota(jnp.int32, (PAIR, PAIR), 1)
    bdmask = (r2 >= HEAD_DIM) == (c2 >= HEAD_DIM)
    ones_bd = bdmask.astype(BF16)
    tr = lax.broadcasted_iota(jnp.int32, (L, L), 0)
    tc = lax.broadcasted_iota(jnp.int32, (L, L), 1)
    tri = (tc <= tr).astype(BF16)

    dot = lambda a, b: _dot(a.astype(BF16), b.astype(BF16))
    dot_nt = lambda a, b: _dot_nt(a.astype(BF16), b.astype(BF16))
    dot_tn = lambda a, b: _dot_tn(a.astype(BF16), b.astype(BF16))

    def bd(x):
        return jnp.where(bdmask, jnp.concatenate([x, x], axis=0), jnp.zeros((), x.dtype))

    def packdiag(full):
        return jnp.where(head0, full[0:L, :], full[L:2 * L, :])

    def each(f, *lists):
        return [f(*xs) for xs in zip(*lists)]

    def rows(c):
        return slice(c * L, (c + 1) * L)

    def lanes(p):
        return slice(p * PAIR, (p + 1) * PAIR)

    def prep(units, out):
        ld = lambda ref: [ref[0, rows(c), lanes(p)].astype(F32) for p, c in units]
        r, lw, k, v, kk, kb = ld(r_ref), ld(lw_ref), ld(k_ref), ld(v_ref), ld(kk_ref), ld(kb_ref)

        cs = each(lambda x: _dot(tri, jnp.concatenate(_hi_lo(x), axis=1)), lw)
        cs = each(lambda x: x[:, 0:PAIR] + x[:, PAIR:2 * PAIR], cs)
        yield
        cs_end = each(lambda x: x[L - 1:L, :], cs)
        kkd = each(lambda kk_, cs_, lw_: kk_ * jnp.exp(cs_ - lw_), kk, cs, lw)
        rd = each(lambda r_, cs_: r_ * jnp.exp(cs_), r, cs)
        p_inv = each(lambda cs_: jnp.exp(-cs_), cs)
        p_end = each(lambda ce, cs_: jnp.exp(ce - cs_), cs_end, cs)
        kh = each(lambda a, b: a * b, k, p_inv)
        bh = each(lambda a, b: a * b, kb, p_inv)
        ke = each(lambda a, b: a * b, k, p_end)
        be = each(lambda a, b: a * b, kb, p_end)

        def scores(kkd_, rd_, kh_, bh_):
            lhs = jnp.concatenate([kkd_, rd_], axis=0)
            rhs_t = jnp.concatenate([jnp.where(head0, kh_, 0.0), jnp.where(head0, 0.0, kh_),
                                     jnp.where(head0, bh_, 0.0), jnp.where(head0, 0.0, bh_)], axis=0)
            return dot_nt(lhs, rhs_t)

        aa = each(scores, kkd, rd, kh, bh)
        yield
        a1 = each(lambda x: jnp.where(strict, x[0:L, 0:PAIR], 0.0), aa)
        a2 = each(lambda x: jnp.where(strict, x[0:L, PAIR:2 * PAIR], 0.0), aa)
        a3 = each(lambda x: jnp.where(incl, x[L:2 * L, 0:PAIR], 0.0), aa)
        a4 = each(lambda x: jnp.where(incl, x[L:2 * L, PAIR:2 * PAIR], 0.0), aa)

        tm = each(lambda n: eye - jnp.where(halves[0], n, 0.0), a2)
        av = each(lambda x1, x3, v_: dot(jnp.concatenate([x1, x3], axis=0), bd(v_)), a1, a3, v)
        yield
        for mask in halves[1:]:
            y = each(lambda n, t_: dot(jnp.where(mask, n, 0.0), bd(t_)), a2, tm)
            yield
            tm = each(lambda t_, y_: t_ - dot(t_, bd(y_)), tm, y)
            yield
        wu = each(lambda t_, kkd_, av_: dot(t_, jnp.concatenate([bd(kkd_), bd(av_[0:L, :])], axis=1)), tm, kkd, av)
        yield
        aw = each(lambda x4, wu_: dot(x4, jnp.concatenate([bd(wu_[:, 0:PAIR]), bd(wu_[:, PAIR:2 * PAIR])], axis=1)),
                  a4, wu)
        gm = each(lambda wu_, be_: dot_tn(wu_[:, 0:PAIR], be_), wu, be)
        yield
        cm = each(lambda v_, ke_, wu_, be_: dot_tn(jnp.concatenate([v_, wu_[:, PAIR:2 * PAIR]], axis=0),
                                                   jnp.concatenate([ke_, -be_], axis=0)), v, ke, wu, be)
        out["pm"] = each(lambda rd_, aw_: (rd_ - aw_[:, 0:PAIR]).astype(BF16), rd, aw)
        out["q"] = each(lambda av_, aw_: av_[L:2 * L, :] - aw_[:, PAIR:2 * PAIR], av, aw)
        out["g"] = each(lambda gm_: jnp.where(bdmask, -gm_, 0.0).astype(BF16), gm)
        out["cm"] = each(packdiag, cm)
        out["gam"] = each(jnp.exp, cs_end)
        yield

    def tail(units, fac, state):
        sb = []
        for j, (p, c) in enumerate(units):
            s = state[p]
            sb.append(s.astype(BF16))
            state[p] = s * fac["gam"][j] + _dot(sb[-1], fac["g"][j]) + fac["cm"][j]
            if j + 1 == len(units) or units[j + 1][1] != c:
                yield
        o = [_dot_nt(pm_, bd(sb_)) + q_ for pm_, sb_, q_ in zip(fac["pm"], sb, fac["q"])]
        rk = [r_ref[0, rows(c), lanes(p)].astype(F32) * k_ref[0, rows(c), lanes(p)].astype(F32) * rk_ref[:, lanes(p)]
              for p, c in units]
        yield
        sums = each(lambda o_, rk_: _dot(jnp.concatenate([o_.astype(BF16), rk_.astype(BF16)], axis=0), ones_bd),
                    o, rk)
        yield
        dlt = each(lambda o_, s_: o_ - s_[0:L, :] * (1.0 / HEAD_DIM), o, sums)
        var = each(lambda d: _dot((d * d).astype(BF16), ones_bd) * (1.0 / HEAD_DIM), dlt)
        yield
        for (p, c), d, vr, s_ in zip(units, dlt, var, sums):
            on = d * lax.rsqrt(vr + GN_EPS) * lnw_ref[:, lanes(p)] + lnb_ref[:, lanes(p)]
            bonus = s_[L:2 * L, :] * v_ref[0, rows(c), lanes(p)].astype(F32)
            o_ref[0, rows(c), lanes(p)] = ((on + bonus) * g_ref[0, rows(c), lanes(p)]).astype(o_ref.dtype)
        yield

    def run(*gens):
        gens = list(gens)
        while gens:
            for gen in list(gens):
                try:
                    next(gen)
                except StopIteration:
                    gens.remove(gen)

    assert sum(groups) == nc and npair % set_pairs == 0, (groups, nc, npair, set_pairs)
    starts = [sum(groups[:gi]) for gi in range(len(groups))]
    items = [[(p, st + j) for j in range(sz) for p in range(p0, p0 + set_pairs)]
             for p0 in range(0, npair, set_pairs) for st, sz in zip(starts, groups)]
    state = [jnp.zeros((L, PAIR), F32) for _ in range(npair)]
    fac = {}
    run(prep(items[0], fac))
    for i, item in enumerate(items):
        nxt = {}
        if i + 1 < len(items):
            run(tail(item, fac, state), prep(items[i + 1], nxt))
        else:
            run(tail(item, fac, state))
        fac = nxt


def _wkv(r, lw, k, v, kk, kb, g, ln_w, ln_b, r_k, npair, set_pairs, groups):
    b, t, dr = r.shape
    nc = t // CHUNK
    wl = npair * PAIR
    seq = pl.BlockSpec((1, t, wl), lambda i, j: (i, 0, j))
    par = pl.BlockSpec((1, wl), lambda i, j: (0, j))
    return pl.pallas_call(
        functools.partial(_wkv_kernel, nc, npair, set_pairs, groups),
        grid=(b, dr // wl),
        in_specs=[seq] * 7 + [par] * 3,
        out_specs=seq,
        out_shape=jax.ShapeDtypeStruct((b, t, dr), BF16),
        compiler_params=pltpu.CompilerParams(dimension_semantics=("arbitrary", "arbitrary"),
                                             vmem_limit_bytes=VMEM_LIMIT),
        name="wkv",
    )(r, lw, k, v, kk, kb, g, ln_w, ln_b, r_k)


def _out_ffn_kernel(final_norm, cf, ahead, n_sub, x_ref, ya_ref, yb_ref, ma_ref, mb_ref, pa_ref, pb_ref, wo_ref,
                    g_ref, wu_ref, cw_ref, cb_ref, wd_ref, gf_ref, o_ref, zcar):
    @pl.when(pl.program_id(1) == 0)
    def _():
        zcar[...] = jnp.zeros(zcar.shape, F32)

    ts = x_ref.shape[1] // n_sub
    subs = [slice(i * ts, (i + 1) * ts) for i in range(n_sub)]
    pa = [_dot(ya_ref[0, s, :], pa_ref[...]) for s in subs]
    pb = [_dot(yb_ref[0, s, :], pb_ref[...]) for s in subs]
    acc = []
    for s, a, b in zip(subs, pa, pb):
        merged = ma_ref[0, s, :].astype(F32) * a + mb_ref[0, s, :].astype(F32) * b
        acc.append(x_ref[0, s, :] + _dot(merged.astype(BF16), wo_ref[...]))
    hb = [_rms(x, g_ref[...]).astype(BF16) for x in acc]
    f = wd_ref.shape[0]
    tasks = [(i, c) for i in range(n_sub) for c in range(f // cf)]

    def up(i, c):
        return [(_dot(hb[i], wu_ref[:, lo:lo + cf]), lo) for lo in (c * cf, f + c * cf)]

    def conv(z, lo):
        z1, z2 = _prev_rows(z, zcar, lo, lo + cf)
        cw = cw_ref[:, lo:lo + cf]
        return cw[0:1, :] * z2 + cw[1:2, :] * z1 + cw[2:3, :] * z + cb_ref[:, lo:lo + cf]

    pending = [up(*task) for task in tasks[:ahead]]
    for n, (i, c) in enumerate(tasks):
        if n + ahead < len(tasks):
            pending.append(up(*tasks[n + ahead]))
        zg, zu = [conv(z, lo) for z, lo in pending.pop(0)]
        act = zg * _sigmoid(zg) * zu
        acc[i] = acc[i] + _dot(act.astype(BF16), wd_ref[c * cf:(c + 1) * cf, :])
    for s, a in zip(subs, acc):
        o_ref[0, s, :] = _rms(a, gf_ref[...]) if final_norm else a


def _out_ffn(x, ya, yb, ma, mb, pa, pb, wo, norm_g, w_up, conv_w, conv_b, w_down, final_g, final_norm, tm, cf, ahead,
             n_sub):
    b, t, d = x.shape
    dr = ya.shape[-1]
    row = lambda w: pl.BlockSpec((1, tm, w), lambda i, j: (i, j, 0))
    consts = [pa, pb, wo, norm_g, w_up, conv_w, conv_b, w_down, final_g]
    return pl.pallas_call(
        functools.partial(_out_ffn_kernel, final_norm, cf, ahead, n_sub),
        grid=(b, t // tm),
        in_specs=[row(d), row(dr), row(dr), row(d), row(d)] + [_const_spec(c.shape) for c in consts],
        out_specs=row(d),
        out_shape=jax.ShapeDtypeStruct((b, t, d), F32),
        scratch_shapes=[pltpu.VMEM((CARRY, w_up.shape[1]), F32)],
        compiler_params=pltpu.CompilerParams(dimension_semantics=("arbitrary", "arbitrary"),
                                             vmem_limit_bytes=VMEM_LIMIT),
        name="out_ffn_final" if final_norm else "out_ffn",
    )(x, ya, yb, ma, mb, *consts)


def _pad_rows(w, rows, offset=0):
    out = jnp.zeros((rows, w.shape[1]), w.dtype)
    return out.at[offset:offset + w.shape[0]].set(w)


def _forward(x, norm_mix_g, w_in, mu_shift, w0, decay_up, a0, a_up, g_up, k_k, k_a, r_k, ln_x_w, ln_x_b,
             vres_down, vres_up, v0, conv_w, proj_a, proj_b, w_out, norm_ffn_g, w_up, ffn_conv_w, ffn_conv_b,
             w_down, norm_final_g, *, mix_cfg, ffn_cfg, wkv_cfg):
    b, t, d = x.shape
    depth = w_in.shape[0]
    dr = w0.shape[-1]
    head = jnp.arange(2 * PAIR) // HEAD_DIM
    ones_h = (head[:, None] == head[None, :]).astype(BF16)
    row = lambda p: p.reshape(1, -1)
    v_first = None
    for l in range(depth):
        win = w_in[l].astype(BF16)
        du = _pad_rows(decay_up[l], LANE).astype(BF16)
        au = _pad_rows(a_up[l], LANE, DECAY_LORA).astype(BF16)
        vres = None
        if l > 0:
            vd = jnp.pad(vres_down[l - 1], ((0, 0), (0, LANE - VRES_LORA))).astype(BF16)
            vu = _pad_rows(vres_up[l - 1], LANE).astype(BF16)
            vres = (vd, vu, row(v0[l - 1]), v_first)
        r, lw, k, v, kk, kb, g, yb, ma, mb = _mix_in(
            x, row(norm_mix_g[l]), win, row(mu_shift[l]), row(w0[l]), du,
            row(a0[l]), au, g_up[l].astype(BF16), row(k_k[l]), row(k_a[l]), conv_w[l], ones_h, vres, *mix_cfg)
        if l == 0:
            v_first = v
        ya = _wkv(r, lw, k, v, kk, kb, g, row(ln_x_w[l]), row(ln_x_b[l]), row(r_k[l]), *wkv_cfg)
        x = _out_ffn(x, ya, yb, ma, mb, proj_a[l].astype(BF16), proj_b[l].astype(BF16), w_out[l].astype(BF16),
                     row(norm_ffn_g[l]), w_up[l].astype(BF16), ffn_conv_w[l], row(ffn_conv_b[l]),
                     w_down[l].astype(BF16), row(norm_final_g), l == depth - 1, *ffn_cfg)
    return x


def kernel(x, norm_mix_g, w_in, mu_shift, w0, decay_up, a0, a_up, g_up, k_k, k_a, r_k, ln_x_w, ln_x_b, vres_down, vres_up, v0, conv_w, proj_a, proj_b, w_out, norm_ffn_g, w_up, ffn_conv_w, ffn_conv_b, w_down, norm_final_g):
    return _forward(x, norm_mix_g, w_in, mu_shift, w0, decay_up, a0, a_up, g_up, k_k, k_a, r_k, ln_x_w, ln_x_b,
                    vres_down, vres_up, v0, conv_w, proj_a, proj_b, w_out, norm_ffn_g, w_up, ffn_conv_w,
                    ffn_conv_b, w_down, norm_final_g,
                    mix_cfg=(512, 2), ffn_cfg=(512, 256, 3, 2), wkv_cfg=(2, 2, (8, 8, 8, 8)))
```
